```python
import jax, jax.numpy as jnp
from jax import lax
import numpy as np

D_MODEL = 1024
BATCH = 8
SEQ = 4096
DEPTH = 1

PLE_DIM = 256
NORM_EPS = 1e-6
MIX_WIDTH = D_MODEL
GLA_WIDTH = MIX_WIDTH // 2
GLA_HEADS = 4
GLA_DV = GLA_WIDTH // GLA_HEADS
GLA_DK = GLA_DV // 2
GLA_GATE_RANK = 16
GLA_TAU = 16.0
GLA_CHUNK = 64
DSW_WIDTH = MIX_WIDTH - GLA_WIDTH
DSW_HEAD_DIM = 64
DSW_HEADS = DSW_WIDTH // DSW_HEAD_DIM
DSW_PATTERNS = ((128, 1), (512, 4), (2048, 16))
DSW_BLOCK = 128
ROT_DIM = DSW_HEAD_DIM // 4
ROPE_THETA = 500000.0
PEER_N_KEYS = 128
PEER_N_EXPERTS = PEER_N_KEYS * PEER_N_KEYS
PEER_HEADS = 8
PEER_TOPK = 16
PEER_QDIM = 256
PEER_HALF = PEER_QDIM // 2
PEER_TOKEN_BLOCK = 128

IN_SPLITS = (GLA_HEADS * GLA_DK, GLA_HEADS * GLA_DK, GLA_WIDTH, GLA_WIDTH, GLA_GATE_RANK,
             DSW_WIDTH, DSW_WIDTH, DSW_WIDTH)
IN_COLS = sum(IN_SPLITS)
IN_SPLIT_IDX = [int(c) for c in np.cumsum(IN_SPLITS)[:-1]]

kernel_name = "hymba_gla_dilated_peer_ple"


def rms_norm(x, g):
    xf = x.astype(jnp.float32)
    y = xf * lax.rsqrt(jnp.mean(xf * xf, axis=-1, keepdims=True) + NORM_EPS)
    return (y * g.astype(jnp.float32)).astype(x.dtype)


def partial_rope(t, positions):
    inv_freq = ROPE_THETA ** (-jnp.arange(0, ROT_DIM, 2, dtype=jnp.float32) / ROT_DIM)
    ang = positions.astype(jnp.float32)[..., None] * inv_freq
    ang = jnp.concatenate([ang, ang], axis=-1)[:, :, None, :]
    rot = t[..., :ROT_DIM].astype(jnp.float32)
    half = ROT_DIM // 2
    rot_half = jnp.concatenate([-rot[..., half:], rot[..., :half]], axis=-1)
    rot = rot * jnp.cos(ang) + rot_half * jnp.sin(ang)
    return jnp.concatenate([rot.astype(t.dtype), t[..., ROT_DIM:]], axis=-1)


def gla_chunked(q, k, v, log_a):
    bsz, slen, nh, dk = q.shape
    dv = v.shape[-1]
    n = slen // GLA_CHUNK

    def chunks(t):
        return t.astype(jnp.float32).reshape(bsz, n, GLA_CHUNK, nh, t.shape[-1]).transpose(0, 3, 1, 2, 4)

    qc = chunks(q) * (dk ** -0.5)
    kc = chunks(k)
    vc = chunks(v)
    b = jnp.cumsum(chunks(log_a), axis=3)
    b_last = b[:, :, :, -1:, :]
    q_dec = qc * jnp.exp(b)
    k_dec = kc * jnp.exp(-b)
    k_end = kc * jnp.exp(b_last - b)
    causal = jnp.tril(jnp.ones((GLA_CHUNK, GLA_CHUNK), dtype=bool))
    attn = jnp.where(causal, jnp.einsum('bhncd,bhnsd->bhncs', q_dec, k_dec), 0.0)
    o_intra = jnp.einsum('bhncs,bhnsv->bhncv', attn, vc)
    chunk_kv = jnp.einsum('bhncd,bhncv->bhndv', k_end, vc)
    decay = jnp.exp(b_last[:, :, :, 0, :])

    def step(state, inp):
        dec, kv = inp
        return dec[..., None] * state + kv, state

    init = jnp.zeros((bsz, nh, dk, dv), jnp.float32)
    _, s_prev = lax.scan(step, init, (jnp.moveaxis(decay, 2, 0), jnp.moveaxis(chunk_kv, 2, 0)))
    s_prev = jnp.moveaxis(s_prev, 0, 2)
    o = o_intra + jnp.einsum('bhncd,bhndv->bhncv', q_dec, s_prev)
    return o.transpose(0, 2, 3, 1, 4).reshape(bsz, slen, nh, dv)


def dilated_window_attention(q, k, v, window, dilation):
    bsz, slen, nh, hd = q.shape
    n_back = window // dilation
    span = dilation * DSW_BLOCK
    sp = -(-slen // span) * span
    pad = sp - slen
    m = sp // dilation
    nb = m // DSW_BLOCK

    def to_blocks(t):
        t = jnp.pad(t, ((0, 0), (0, pad), (0, 0), (0, 0)))
        t = t.reshape(bsz, m, dilation, nh, hd).transpose(0, 2, 1, 3, 4)
        return t.reshape(bsz, dilation, nb, DSW_BLOCK, nh, hd)

    qb, kb, vb = to_blocks(q), to_blocks(k), to_blocks(v)
    zpad = ((0, 0), (0, 0), (1, 0), (0, 0), (0, 0), (0, 0))
    kk = jnp.concatenate([jnp.pad(kb[:, :, :-1], zpad), kb], axis=3)
    vv = jnp.concatenate([jnp.pad(vb[:, :, :-1], zpad), vb], axis=3)
    scores = jnp.einsum('brnqhd,brnkhd->brnhqk', qb, kk).astype(jnp.float32) * (hd ** -0.5)
    qi = jnp.arange(DSW_BLOCK)[:, None]
    ki = jnp.arange(2 * DSW_BLOCK)[None, :]
    dist = qi + DSW_BLOCK - ki
    band = (dist >= 0) & (dist <= n_back)
    no_prev = (jnp.arange(nb) == 0)[:, None, None] & (ki < DSW_BLOCK)[None]
    mask = (band[None] & ~no_prev)[:, None]
    scores = jnp.where(mask, scores, -jnp.inf)
    lse = jax.nn.logsumexp(scores, axis=-1)
    probs = jnp.exp(scores - lse[..., None])
    o = jnp.einsum('brnhqk,brnkhd->brnqhd', probs.astype(v.dtype), vv)
    o = o.reshape(bsz, dilation, m, nh, hd).transpose(0, 2, 1, 3, 4).reshape(bsz, sp, nh, hd)[:, :slen]
    lse = lse.transpose(0, 1, 2, 4, 3).reshape(bsz, dilation, m, nh).transpose(0, 2, 1, 3)
    lse = lse.reshape(bsz, sp, nh)[:, :slen]
    return o, lse


def dilated_mixture(q, k, v):
    outs, lses = [], []
    for window, dilation in DSW_PATTERNS:
        o, lse = dilated_window_attention(q, k, v, window, dilation)
        outs.append(o.astype(jnp.float32))
        lses.append(lse)
    w = jax.nn.softmax(jnp.stack(lses, axis=0), axis=0)
    o = jnp.sum(w[..., None] * jnp.stack(outs, axis=0), axis=0)
    return o.astype(q.dtype)


def peer(h, w_q, sub_keys, u_table, v_table):
    bsz, slen, dm = h.shape
    t = bsz * slen
    hf = h.reshape(t, dm)
    qr = (hf @ w_q).reshape(t, PEER_HEADS, 2, PEER_HALF)
    sc = jnp.einsum('thcd,hcnd->thcn', qr, sub_keys).astype(jnp.float32)
    s_top, i_top = lax.top_k(sc, PEER_TOPK)
    cand = (s_top[:, :, 0, :, None] + s_top[:, :, 1, None, :]).reshape(t, PEER_HEADS, PEER_TOPK * PEER_TOPK)
    cand_idx = (i_top[:, :, 0, :, None] * PEER_N_KEYS + i_top[:, :, 1, None, :]).reshape(t, PEER_HEADS, PEER_TOPK * PEER_TOPK)
    best, pos = lax.top_k(cand, PEER_TOPK)
    expert_idx = jnp.take_along_axis(cand_idx, pos, axis=-1)
    gates = jax.nn.softmax(best, axis=-1)
    nblk = t // PEER_TOKEN_BLOCK

    def block(args):
        xb, ib, gb = args
        ub = jnp.take(u_table, ib, axis=0)
        a = jnp.einsum('td,thkd->thk', xb, ub).astype(jnp.float32)
        act = (jax.nn.gelu(a, approximate=False) * gb).astype(xb.dtype)
        vb = jnp.take(v_table, ib, axis=0)
        return jnp.einsum('thk,thkd->td', act, vb)

    out = lax.map(block, (hf.reshape(nblk, PEER_TOKEN_BLOCK, dm),
                          expert_idx.reshape(nblk, PEER_TOKEN_BLOCK, PEER_HEADS, PEER_TOPK),
                          gates.reshape(nblk, PEER_TOKEN_BLOCK, PEER_HEADS, PEER_TOPK)))
    return out.reshape(bsz, slen, dm)


def setup_inputs(seed: int = 0) -> dict:
    key = jax.random.key(seed)
    ks = jax.random.split(key, 20)
    f32 = jnp.float32
    nrm = lambda k, shape, s: jax.random.normal(k, shape, f32) * s
    gain = lambda k, shape: 1.0 + 0.02 * jax.random.normal(k, shape, f32)
    x = nrm(ks[0], (BATCH, SEQ, D_MODEL), 1.0)
    p = nrm(ks[1], (DEPTH, BATCH, SEQ, PLE_DIM), 1.0)
    positions = (jnp.arange(SEQ, dtype=jnp.int32)[None, :]
                 + jax.random.randint(ks[2], (BATCH, 1), 0, 1024, dtype=jnp.int32))
    return {
        "x": x,
        "p": p,
        "positions": positions,
        "g_mix": gain(ks[3], (DEPTH, D_MODEL)),
        "w_in": nrm(ks[4], (DEPTH, D_MODEL, IN_COLS), D_MODEL ** -0.5),
        "w_gla_gate_up": nrm(ks[5], (DEPTH, GLA_GATE_RANK, GLA_HEADS * GLA_DK), GLA_GATE_RANK ** -0.5),
        "b_gla_gate": nrm(ks[6], (DEPTH, GLA_HEADS * GLA_DK), 0.1),
        "g_gla_out": gain(ks[7], (DEPTH, GLA_HEADS, GLA_DV)),
        "w_out": nrm(ks[8], (DEPTH, MIX_WIDTH, D_MODEL), MIX_WIDTH ** -0.5),
        "g_peer": gain(ks[9], (DEPTH, D_MODEL)),
        "w_peer_q": nrm(ks[10], (DEPTH, D_MODEL, PEER_HEADS * PEER_QDIM), D_MODEL ** -0.5),
        "peer_sub_keys": nrm(ks[11], (DEPTH, PEER_HEADS, 2, PEER_N_KEYS, PEER_HALF), PEER_HALF ** -0.5),
        "peer_u": nrm(ks[12], (DEPTH, PEER_N_EXPERTS, D_MODEL), D_MODEL ** -0.5),
        "peer_v": nrm(ks[13], (DEPTH, PEER_N_EXPERTS, D_MODEL), 0.5),
        "g_ple": gain(ks[14], (DEPTH, D_MODEL)),
        "w_ple_gate": nrm(ks[15], (DEPTH, D_MODEL, D_MODEL), D_MODEL ** -0.5),
        "w_ple_proj": nrm(ks[16], (DEPTH, PLE_DIM, D_MODEL), PLE_DIM ** -0.5),
        "g_final": gain(ks[17], (D_MODEL,)),
    }


def reference(x, p, positions, g_mix, w_in, w_gla_gate_up, b_gla_gate, g_gla_out, w_out,
              g_peer, w_peer_q, peer_sub_keys, peer_u, peer_v, g_ple, w_ple_gate, w_ple_proj, g_final):
    bsz, slen, _ = x.shape
    for i in range(DEPTH):
        h = rms_norm(x, g_mix[i])
        proj = h @ w_in[i]
        qa, ka, va, ra, ga_low, qb, kb, vb = jnp.split(proj, IN_SPLIT_IDX, axis=-1)
        log_a = jax.nn.log_sigmoid((ga_low @ w_gla_gate_up[i] + b_gla_gate[i]).astype(jnp.float32)) / GLA_TAU
        oa = gla_chunked(qa.reshape(bsz, slen, GLA_HEADS, GLA_DK),
                         ka.reshape(bsz, slen, GLA_HEADS, GLA_DK),
                         va.reshape(bsz, slen, GLA_HEADS, GLA_DV),
                         log_a.reshape(bsz, slen, GLA_HEADS, GLA_DK)).astype(x.dtype)
        oa = rms_norm(oa, g_gla_out[i]).reshape(bsz, slen, GLA_WIDTH) * jax.nn.silu(ra)
        qb = partial_rope(qb.reshape(bsz, slen, DSW_HEADS, DSW_HEAD_DIM), positions)
        kb = partial_rope(kb.reshape(bsz, slen, DSW_HEADS, DSW_HEAD_DIM), positions)
        vb = vb.reshape(bsz, slen, DSW_HEADS, DSW_HEAD_DIM)
        ob = dilated_mixture(qb, kb, vb).reshape(bsz, slen, DSW_WIDTH)
        x = x + jnp.concatenate([oa, ob], axis=-1) @ w_out[i]
        x = x + peer(rms_norm(x, g_peer[i]), w_peer_q[i], peer_sub_keys[i], peer_u[i], peer_v[i])
        gate = jax.nn.sigmoid(rms_norm(x, g_ple[i]) @ w_ple_gate[i])
        x = x + (p[i] @ w_ple_proj[i]) * gate
    return rms_norm(x, g_final)
```

```python
import functools
import math

import jax
import jax.numpy as jnp
import numpy as np
from jax import lax
from jax.experimental import pallas as pl
from jax.experimental.pallas import tpu as pltpu

F32 = jnp.float32
BF16 = jnp.bfloat16

D_MODEL = 1024
PLE_DIM = 256
NORM_EPS = 1e-6
GLA_HEADS = 4
GLA_DK = 64
GLA_DV = 128
GLA_QK = GLA_HEADS * GLA_DK
GLA_WIDTH = GLA_HEADS * GLA_DV
GLA_GATE_RANK = 16
GLA_TAU = 16.0
GLA_CHUNK = 64
DSW_HEAD_DIM = 64
DSW_HEADS = 8
DSW_WIDTH = DSW_HEADS * DSW_HEAD_DIM
DSW_PATTERNS = ((128, 1), (512, 4), (2048, 16))
DSW_BLOCK = 128
ROT_DIM = 16
ROPE_THETA = 500000.0
PEER_N_KEYS = 128
PEER_HEADS = 8
PEER_TOPK = 16
PEER_HALF = 128
PEER_QCOLS = PEER_HEADS * 2 * PEER_HALF
PEER_PICKS = PEER_HEADS * PEER_TOPK

LANES = 128
SUBLANES = 8
GATE_PAD = LANES
GLA_IN_COLS = 2 * GLA_QK + 2 * GLA_WIDTH + GATE_PAD
ATT_IN_COLS = 3 * DSW_WIDTH

VMEM_LIMIT = 56 * 1024 * 1024


def _cparams(*sem):
    return pltpu.CompilerParams(dimension_semantics=sem, vmem_limit_bytes=VMEM_LIMIT)


def _rms(x, g):
    return x * lax.rsqrt(jnp.mean(x * x, axis=-1, keepdims=True) + NORM_EPS) * g


IN_TM = 512


def _in_proj_kernel(x_ref, pos_ref, g_ref, wg_ref, wa_ref, invf_ref, sga_ref, sgb_ref,
                    gla_ref, att_ref):
    h = _rms(x_ref[...], g_ref[...]).astype(BF16)
    gla_ref[...] = jnp.dot(h, wg_ref[...], preferred_element_type=F32)
    ang = pos_ref[...].astype(F32) * invf_ref[...]
    cos = jnp.cos(ang)
    sin = jnp.sin(ang)
    sin_a = sin * sga_ref[...]
    sin_b = sin * sgb_ref[...]
    half = ROT_DIM // 2
    for c in range(ATT_IN_COLS // LANES):
        y = jnp.dot(h, wa_ref[:, c * LANES:(c + 1) * LANES], preferred_element_type=F32)
        if c < 2 * DSW_WIDTH // LANES:
            y = (y * cos + pltpu.roll(y, LANES - half, axis=1) * sin_a
                 + pltpu.roll(y, half, axis=1) * sin_b)
        att_ref[:, c * LANES:(c + 1) * LANES] = y


def _in_proj(x2, pos2, g, wg, wa, invf, sga, sgb):
    t = x2.shape[0]
    full = lambda shape: pl.BlockSpec(shape, lambda i: (0, 0))
    return pl.pallas_call(
        _in_proj_kernel,
        grid=(t // IN_TM,),
        in_specs=[pl.BlockSpec((IN_TM, D_MODEL), lambda i: (i, 0)),
                  pl.BlockSpec((IN_TM, 1), lambda i: (i, 0)),
                  full((1, D_MODEL)), full((D_MODEL, GLA_IN_COLS)), full((D_MODEL, ATT_IN_COLS)),
                  full((1, LANES)), full((1, LANES)), full((1, LANES))],
        out_specs=[pl.BlockSpec((IN_TM, GLA_IN_COLS), lambda i: (i, 0)),
                   pl.BlockSpec((IN_TM, ATT_IN_COLS), lambda i: (i, 0))],
        out_shape=[jax.ShapeDtypeStruct((t, GLA_IN_COLS), F32),
                   jax.ShapeDtypeStruct((t, ATT_IN_COLS), F32)],
        compiler_params=_cparams("parallel"),
        name="in_proj",
    )(x2, pos2, g, wg, wa, invf, sga, sgb)


GLA_SB = 512


def _log_sigmoid(z):
    return jnp.minimum(z, 0.0) - jnp.log(1.0 + jnp.exp(-jnp.abs(z)))


def _gla_kernel(in_ref, wup_ref, bup_ref, gout_ref, o_ref, state_ref):
    @pl.when(pl.program_id(1) == 0)
    def _():
        state_ref[...] = jnp.zeros_like(state_ref)

    c_i = lax.broadcasted_iota(jnp.int32, (GLA_CHUNK, GLA_CHUNK), 0)
    c_j = lax.broadcasted_iota(jnp.int32, (GLA_CHUNK, GLA_CHUNK), 1)
    causal = c_j <= c_i
    tri = causal.astype(F32)
    o_q, o_k, o_v, o_r, o_g = 0, GLA_QK, 2 * GLA_QK, 2 * GLA_QK + GLA_WIDTH, 2 * GLA_QK + 2 * GLA_WIDTH

    def chunk(c, carry):
        r0 = pl.multiple_of(c * GLA_CHUNK, GLA_CHUNK)
        rows = pl.ds(r0, GLA_CHUNK)
        code = in_ref[0, rows, o_g:o_g + GATE_PAD]
        pre = jnp.dot(code, wup_ref[...], preferred_element_type=F32) + bup_ref[...]
        log_a = _log_sigmoid(pre) * (1.0 / GLA_TAU)
        b = jnp.dot(tri, log_a, preferred_element_type=F32,
                    precision=lax.Precision.HIGHEST)
        b_last = b[GLA_CHUNK - 1:GLA_CHUNK, :]
        q = in_ref[0, rows, o_q:o_q + GLA_QK]
        k = in_ref[0, rows, o_k:o_k + GLA_QK]
        q_dec = q * (GLA_DK ** -0.5) * jnp.exp(b)
        k_dec = k * jnp.exp(-b)
        k_end = k * jnp.exp(b_last - b)
        decay = jnp.exp(b_last)
        for h in range(GLA_HEADS):
            ks = slice(h * GLA_DK, (h + 1) * GLA_DK)
            vs = slice(h * GLA_DV, (h + 1) * GLA_DV)
            v = in_ref[0, rows, o_v + h * GLA_DV:o_v + (h + 1) * GLA_DV]
            r = in_ref[0, rows, o_r + h * GLA_DV:o_r + (h + 1) * GLA_DV]
            qd = q_dec[:, ks]
            attn = lax.dot_general(qd, k_dec[:, ks], (((1,), (1,)), ((), ())),
                                   preferred_element_type=F32)
            attn = jnp.where(causal, attn, 0.0)
            st = state_ref[h]
            o = (jnp.dot(attn, v, preferred_element_type=F32)
                 + lax.dot_general(qd, st, (((1,), (1,)), ((), ())), preferred_element_type=F32))
            kv_t = lax.dot_general(v, k_end[:, ks], (((0,), (0,)), ((), ())),
                                   preferred_element_type=F32)
            state_ref[h] = st * decay[:, ks] + kv_t
            o = _rms(o, gout_ref[h:h + 1, :])
            o_ref[0, rows, vs] = o * (r / (1.0 + jnp.exp(-r)))
        return carry

    lax.fori_loop(0, GLA_SB // GLA_CHUNK, chunk, 0)


def _gla(gla_in, wup, bup, gout):
    b, s, _ = gla_in.shape
    full = lambda shape: pl.BlockSpec(shape, lambda i, j: (0,) * len(shape))
    return pl.pallas_call(
        _gla_kernel,
        grid=(b, s // GLA_SB),
        in_specs=[pl.BlockSpec((1, GLA_SB, GLA_IN_COLS), lambda i, j: (i, j, 0)),
                  full((GATE_PAD, GLA_QK)), full((1, GLA_QK)), full((GLA_HEADS, GLA_DV))],
        out_specs=pl.BlockSpec((1, GLA_SB, GLA_WIDTH), lambda i, j: (i, j, 0)),
        out_shape=jax.ShapeDtypeStruct((b, s, GLA_WIDTH), F32),
        scratch_shapes=[pltpu.VMEM((GLA_HEADS, GLA_DV, GLA_DK), F32)],
        compiler_params=_cparams("parallel", "arbitrary"),
        name="gla",
    )(gla_in, wup, bup, gout)


HEADS_PER_STEP = LANES // DSW_HEAD_DIM


def _dsw_kernel(q_ref, k_ref, v_ref, o_ref, lse_ref, *, n_back):
    m_len = q_ref.shape[1]
    nb = m_len // DSW_BLOCK
    qi = lax.broadcasted_iota(jnp.int32, (DSW_BLOCK, 2 * DSW_BLOCK), 0)
    ki = lax.broadcasted_iota(jnp.int32, (DSW_BLOCK, 2 * DSW_BLOCK), 1)
    rel = qi - ki

    def block(n, carry):
        cur_off = jnp.where(n == 0, 0, DSW_BLOCK)
        q0 = pl.multiple_of(n * DSW_BLOCK, DSW_BLOCK)
        k0 = pl.multiple_of(n * DSW_BLOCK - cur_off, DSW_BLOCK)
        dist = rel + cur_off
        mask = (dist >= 0) & (dist <= n_back)
        q = q_ref[0, pl.ds(q0, DSW_BLOCK), :]
        k = k_ref[0, pl.ds(k0, 2 * DSW_BLOCK), :]
        v = v_ref[0, pl.ds(k0, 2 * DSW_BLOCK), :]
        outs, lses = [], []
        for h in range(HEADS_PER_STEP):
            hs = slice(h * DSW_HEAD_DIM, (h + 1) * DSW_HEAD_DIM)
            s = lax.dot_general(q[:, hs].astype(BF16), k[:, hs].astype(BF16),
                                (((1,), (1,)), ((), ())), preferred_element_type=F32)
            s = jnp.where(mask, s * (DSW_HEAD_DIM ** -0.5), -jnp.inf)
            m = jnp.max(s, axis=-1, keepdims=True)
            p = jnp.exp(s - m)
            l = jnp.sum(p, axis=-1, keepdims=True)
            o = jnp.dot(p.astype(BF16), v[:, hs].astype(BF16), preferred_element_type=F32)
            outs.append(o / l)
            lses.append(jnp.broadcast_to(m + jnp.log(l), (DSW_BLOCK, DSW_HEAD_DIM)))
        o_ref[0, pl.ds(q0, DSW_BLOCK), :] = jnp.concatenate(outs, axis=-1)
        lse_ref[0, pl.ds(q0, DSW_BLOCK), :] = jnp.concatenate(lses, axis=-1)
        return carry

    lax.fori_loop(0, nb, block, 0)


def _dsw(att_in, window, dilation):
    b, s, _ = att_in.shape
    m_len = s // dilation
    n_back = window // dilation
    assert s % (dilation * DSW_BLOCK) == 0 and m_len // DSW_BLOCK >= 2
    view = att_in.reshape(b, m_len, dilation * ATT_IN_COLS)
    per_tok = ATT_IN_COLS // LANES
    n_hp = DSW_WIDTH // LANES
    blk = (1, m_len, LANES)
    in_spec = lambda off: pl.BlockSpec(blk, lambda i, r, hp: (i, 0, r * per_tok + off + hp))
    out_spec = pl.BlockSpec(blk, lambda i, r, hp: (i, 0, r * n_hp + hp))
    out_sds = jax.ShapeDtypeStruct((b, m_len, dilation * DSW_WIDTH), F32)
    o, lse = pl.pallas_call(
        functools.partial(_dsw_kernel, n_back=n_back),
        grid=(b, dilation, n_hp),
        in_specs=[in_spec(0), in_spec(n_hp), in_spec(2 * n_hp)],
        out_specs=[out_spec, out_spec],
        out_shape=[out_sds, out_sds],
        compiler_params=_cparams("parallel", "parallel", "parallel"),
        name=f"dsw_d{dilation}",
    )(view, view, view)
    return o.reshape(b, s, DSW_WIDTH), lse.reshape(b, s, DSW_WIDTH)


MIX_TM = 256
PEER_GROUPS = PEER_HEADS * 2


def _mix_out_kernel(x_ref, oa_ref, o1_ref, o2_ref, o3_ref, l1_ref, l2_ref, l3_ref,
                    wo_ref, gp_ref, wq_ref, sk_ref, x1_ref, h_ref, sc_ref):
    l1, l2, l3 = l1_ref[...], l2_ref[...], l3_ref[...]
    m = jnp.maximum(jnp.maximum(l1, l2), l3)
    e1, e2, e3 = jnp.exp(l1 - m), jnp.exp(l2 - m), jnp.exp(l3 - m)
    ob = (e1 * o1_ref[...] + e2 * o2_ref[...] + e3 * o3_ref[...]) / (e1 + e2 + e3)
    mix = (jnp.dot(oa_ref[...].astype(BF16), wo_ref[:GLA_WIDTH, :], preferred_element_type=F32)
           + jnp.dot(ob.astype(BF16), wo_ref[GLA_WIDTH:, :], preferred_element_type=F32))
    x1 = x_ref[...] + mix
    x1_ref[...] = x1
    h = _rms(x1, gp_ref[...])
    h_ref[...] = h
    hb = h.astype(BF16)
    for g in range(PEER_GROUPS):
        qg = jnp.dot(hb, wq_ref[:, g * PEER_HALF:(g + 1) * PEER_HALF], preferred_element_type=F32)
        sc_ref[g] = lax.dot_general(sk_ref[g], qg.astype(BF16), (((1,), (1,)), ((), ())),
                                    preferred_element_type=F32)


def _mix_out(x2, oa, o1, o2, o3, l1, l2, l3, wo, gp, wq, sk):
    t = x2.shape[0]
    tok = lambda w: pl.BlockSpec((MIX_TM, w), lambda i: (i, 0))
    full = lambda shape: pl.BlockSpec(shape, lambda i: (0,) * len(shape))
    return pl.pallas_call(
        _mix_out_kernel,
        grid=(t // MIX_TM,),
        in_specs=[tok(D_MODEL)] + [tok(DSW_WIDTH)] * 7
                 + [full((D_MODEL, D_MODEL)), full((1, D_MODEL)), full((D_MODEL, PEER_QCOLS)),
                    full((PEER_GROUPS, PEER_N_KEYS, PEER_HALF))],
        out_specs=[tok(D_MODEL), tok(D_MODEL),
                   pl.BlockSpec((PEER_GROUPS, PEER_N_KEYS, MIX_TM), lambda i: (0, 0, i))],
        out_shape=[jax.ShapeDtypeStruct((t, D_MODEL), F32), jax.ShapeDtypeStruct((t, D_MODEL), F32),
                   jax.ShapeDtypeStruct((PEER_GROUPS, PEER_N_KEYS, t), F32)],
        compiler_params=_cparams("parallel"),
        name="mix_out",
    )(x2, oa, o1, o2, o3, l1, l2, l3, wo, gp, wq, sk)


TOPK_TT = 128


def _top_rows(s, payload=None):
    n_rows = s.shape[0]
    rows = lax.broadcasted_iota(jnp.int32, s.shape, 0).astype(F32)
    vals, idxs, pays = [], [], []
    for _ in range(PEER_TOPK):
        m = jnp.max(s, axis=0, keepdims=True)
        i = jnp.min(jnp.where(s == m, rows, float(n_rows)), axis=0, keepdims=True)
        hit = rows == i
        if payload is not None:
            pays.append(jnp.max(jnp.where(hit, payload, -1.0), axis=0, keepdims=True))
        s = jnp.where(hit, -jnp.inf, s)
        vals.append(m)
        idxs.append(i)
    out = (jnp.concatenate(vals, axis=0), jnp.concatenate(idxs, axis=0))
    if payload is not None:
        out = out + (jnp.concatenate(pays, axis=0),)
    return out


def _peer_topk_kernel(sc_ref, idx_ref, gate_ref):
    def head(h, carry):
        v0, i0 = _top_rows(sc_ref[2 * h])
        v1, i1 = _top_rows(sc_ref[2 * h + 1])
        cand = jnp.concatenate([v0[i:i + 1, :] + v1 for i in range(PEER_TOPK)], axis=0)
        cidx = jnp.concatenate([i0[i:i + 1, :] * float(PEER_N_KEYS) + i1 for i in range(PEER_TOPK)],
                               axis=0)
        best, _, expert = _top_rows(cand, cidx)
        e = jnp.exp(best - best[0:1, :])
        gates = e / jnp.sum(e, axis=0, keepdims=True)
        r0 = pl.multiple_of(h * PEER_TOPK, PEER_TOPK)
        idx_ref[pl.ds(r0, PEER_TOPK), :] = expert.astype(jnp.int32)
        gate_ref[pl.ds(r0, PEER_TOPK), :] = gates
        return carry

    lax.fori_loop(0, PEER_HEADS, head, 0)


def _peer_topk(scores_t):
    t = scores_t.shape[-1]
    out_spec = pl.BlockSpec((PEER_PICKS, TOPK_TT), lambda i: (0, i))
    return pl.pallas_call(
        _peer_topk_kernel,
        grid=(t // TOPK_TT,),
        in_specs=[pl.BlockSpec((PEER_GROUPS, PEER_N_KEYS, TOPK_TT), lambda i: (0, 0, i))],
        out_specs=[out_spec, out_spec],
        out_shape=[jax.ShapeDtypeStruct((PEER_PICKS, t), jnp.int32),
                   jax.ShapeDtypeStruct((PEER_PICKS, t), F32)],
        compiler_params=_cparams("parallel"),
        name="peer_topk",
    )(scores_t)


FFN_TB = 64
FFN_NBUF = 8
ROW_SUB = 2 * D_MODEL // LANES
D_SUB = D_MODEL // LANES


def _sublane_fold(vs):
    sub = lax.broadcasted_iota(jnp.int32, (SUBLANES, LANES), 0)
    span = SUBLANES // 2
    while len(vs) > 1:
        low = (sub % (2 * span)) < span
        nxt = []
        for j in range(len(vs) // 2):
            a, b = vs[j], vs[j + len(vs) // 2]
            if span == SUBLANES // 2:
                nxt.append(jnp.where(low, a, b) + pltpu.roll(jnp.where(low, b, a), span, axis=0))
            else:
                nxt.append(jnp.where(low, a + pltpu.roll(a, SUBLANES - span, axis=0),
                                     b + pltpu.roll(b, span, axis=0)))
        vs = nxt
        span //= 2
    return vs[0]


def _peer_ffn_kernel(idx_ref, h_ref, gt_ref, uv_ref, o_ref, buf, act_ref, sems):
    def issue(t, slot):
        for k in range(PEER_PICKS):
            pltpu.make_async_copy(uv_ref.at[idx_ref[t, k]], buf.at[slot, k],
                                  sems.at[slot]).start(priority=k % 2)

    def wait(slot):
        pltpu.make_async_copy(uv_ref.at[pl.ds(0, PEER_PICKS)], buf.at[slot], sems.at[slot]).wait()

    def prime(t, carry):
        issue(t, t)
        return carry

    lax.fori_loop(0, FFN_NBUF - 1, prime, 0)

    lane = lax.broadcasted_iota(jnp.int32, (PEER_PICKS, LANES), 1)
    lane0 = (pl.program_id(0) * FFN_TB) % LANES

    def token(t, carry):
        nxt = t + FFN_NBUF - 1

        @pl.when(nxt < FFN_TB)
        def _():
            issue(nxt, nxt % FFN_NBUF)

        slot = t % FFN_NBUF
        wait(slot)
        x = h_ref[t]
        packed = []
        for g in range(PEER_PICKS // SUBLANES):
            packed.append(_sublane_fold(
                [buf[slot, g * SUBLANES + j, 0:D_SUB, :] * x for j in range(SUBLANES)]))
        part = jnp.concatenate(packed, axis=0)
        a = jnp.sum(part, axis=-1, keepdims=True)
        gate = jnp.sum(jnp.where(lane == lane0 + t, gt_ref[...], 0.0), axis=-1, keepdims=True)
        act = 0.5 * a * (1.0 + lax.erf(a * (2.0 ** -0.5))) * gate
        act_ref[...] = jnp.broadcast_to(act, (PEER_PICKS, LANES))
        acc = jnp.zeros((D_SUB, LANES), F32)
        for k in range(PEER_PICKS):
            acc = acc + act_ref[k:k + 1, :] * buf[slot, k, D_SUB:ROW_SUB, :]
        o_ref[t] = acc
        return carry

    lax.fori_loop(0, FFN_TB, token, 0)


def _peer_ffn(idx, h3, gates_t, uv):
    t = idx.shape[0]
    return pl.pallas_call(
        _peer_ffn_kernel,
        grid=(t // FFN_TB,),
        in_specs=[pl.BlockSpec((FFN_TB, PEER_PICKS), lambda i: (i, 0), memory_space=pltpu.SMEM),
                  pl.BlockSpec((FFN_TB, D_SUB, LANES), lambda i: (i, 0, 0)),
                  pl.BlockSpec((PEER_PICKS, LANES), lambda i: (0, i * FFN_TB // LANES)),
                  pl.BlockSpec(memory_space=pl.ANY)],
        out_specs=pl.BlockSpec((FFN_TB, D_SUB, LANES), lambda i: (i, 0, 0)),
        out_shape=jax.ShapeDtypeStruct((t, D_SUB, LANES), F32),
        scratch_shapes=[pltpu.VMEM((FFN_NBUF, PEER_PICKS, ROW_SUB, LANES), F32),
                        pltpu.VMEM((PEER_PICKS, LANES), F32),
                        pltpu.SemaphoreType.DMA((FFN_NBUF,))],
        compiler_params=_cparams("arbitrary"),
        name="peer_ffn",
    )(idx, h3, gates_t, uv)


FIN_TM = 256


def _final_kernel(x1_ref, peer_ref, p_ref, gple_ref, wg_ref, wp_ref, gfin_ref, o_ref):
    x2 = x1_ref[...] + peer_ref[...]
    z = jnp.dot(_rms(x2, gple_ref[...]).astype(BF16), wg_ref[...], preferred_element_type=F32)
    gate = 1.0 / (1.0 + jnp.exp(-z))
    emb = jnp.dot(p_ref[...].astype(BF16), wp_ref[...], preferred_element_type=F32)
    o_ref[...] = _rms(x2 + emb * gate, gfin_ref[...])


def _final(x1, peer_out, p2, gple, wg, wp, gfin):
    t = x1.shape[0]
    tok = lambda w: pl.BlockSpec((FIN_TM, w), lambda i: (i, 0))
    full = lambda shape: pl.BlockSpec(shape, lambda i: (0, 0))
    return pl.pallas_call(
        _final_kernel,
        grid=(t // FIN_TM,),
        in_specs=[tok(D_MODEL), tok(D_MODEL), tok(PLE_DIM), full((1, D_MODEL)),
                  full((D_MODEL, D_MODEL)), full((PLE_DIM, D_MODEL)), full((1, D_MODEL))],
        out_specs=tok(D_MODEL),
        out_shape=jax.ShapeDtypeStruct((t, D_MODEL), F32),
        compiler_params=_cparams("parallel"),
        name="final",
    )(x1, peer_out, p2, gple, wg, wp, gfin)


def _rope_lane_constants():
    lane = np.arange(LANES) % DSW_HEAD_DIM
    half = ROT_DIM // 2
    inv_freq = ROPE_THETA ** (-jnp.arange(0, ROT_DIM, 2, dtype=F32) / ROT_DIM)
    invf = jnp.where(lane < ROT_DIM, inv_freq[lane % half], 0.0).astype(F32)[None, :]
    sga = jnp.asarray(np.where(lane < half, -1.0, 0.0), F32)[None, :]
    sgb = jnp.asarray(np.where((lane >= half) & (lane < ROT_DIM), 1.0, 0.0), F32)[None, :]
    return invf, sga, sgb


def kernel(x, p, positions, g_mix, w_in, w_gla_gate_up, b_gla_gate, g_gla_out, w_out, g_peer,
           w_peer_q, peer_sub_keys, peer_u, peer_v, g_ple, w_ple_gate, w_ple_proj, g_final):
    bsz, slen, dm = x.shape
    t = bsz * slen
    x2 = x.reshape(t, dm)
    pos2 = positions.reshape(t, 1)

    w = w_in[0]
    c_gate = 2 * GLA_QK + 2 * GLA_WIDTH
    w_gla = jnp.concatenate(
        [w[:, :c_gate], jnp.pad(w[:, c_gate:c_gate + GLA_GATE_RANK], ((0, 0), (0, GATE_PAD - GLA_GATE_RANK)))],
        axis=1).astype(BF16)
    w_att = w[:, c_gate + GLA_GATE_RANK:].astype(BF16)
    invf, sga, sgb = _rope_lane_constants()
    gla_in, att_in = _in_proj(x2, pos2, g_mix[0][None, :], w_gla, w_att, invf, sga, sgb)

    wup = jnp.pad(w_gla_gate_up[0], ((0, GATE_PAD - GLA_GATE_RANK), (0, 0)))
    oa = _gla(gla_in.reshape(bsz, slen, GLA_IN_COLS), wup, b_gla_gate[0][None, :], g_gla_out[0])

    att3 = att_in.reshape(bsz, slen, ATT_IN_COLS)
    pats = [_dsw(att3, window, dilation) for window, dilation in DSW_PATTERNS]
    flat = lambda a: a.reshape(t, a.shape[-1])

    sub_keys = peer_sub_keys[0].reshape(PEER_GROUPS, PEER_N_KEYS, PEER_HALF).astype(BF16)
    x1, h, scores_t = _mix_out(
        x2, flat(oa), flat(pats[0][0]), flat(pats[1][0]), flat(pats[2][0]),
        flat(pats[0][1]), flat(pats[1][1]), flat(pats[2][1]),
        w_out[0].astype(BF16), g_peer[0][None, :], w_peer_q[0].astype(BF16), sub_keys)

    idx_t, gates_t = _peer_topk(scores_t)

    n_exp = peer_u.shape[1]
    uv = jnp.concatenate([peer_u[0].reshape(n_exp, D_SUB, LANES),
                          peer_v[0].reshape(n_exp, D_SUB, LANES)], axis=1)
    peer_out = _peer_ffn(idx_t.T, h.reshape(t, D_SUB, LANES), gates_t, uv)

    out = _final(x1, peer_out.reshape(t, dm), p[0].reshape(t, PLE_DIM), g_ple[0][None, :],
                 w_ple_gate[0].astype(BF16), w_ple_proj[0].astype(BF16), g_final[None, :])
    return out.reshape(bsz, slen, dm)
```

```python
import functools
import math

import jax
import jax.numpy as jnp
import numpy as np
from jax import lax
from jax.experimental import pallas as pl
from jax.experimental.pallas import tpu as pltpu

F32 = jnp.float32
BF16 = jnp.bfloat16

D_MODEL = 1024
PLE_DIM = 256
NORM_EPS = 1e-6
GLA_HEADS = 4
GLA_DK = 64
GLA_DV = 128
GLA_QK = GLA_HEADS * GLA_DK
GLA_WIDTH = GLA_HEADS * GLA_DV
GLA_GATE_RANK = 16
GLA_TAU = 16.0
GLA_CHUNK = 64
DSW_HEAD_DIM = 64
DSW_HEADS = 8
DSW_WIDTH = DSW_HEADS * DSW_HEAD_DIM
DSW_PATTERNS = ((128, 1), (512, 4), (2048, 16))
DSW_BLOCK = 128
ROT_DIM = 16
ROPE_THETA = 500000.0
PEER_N_KEYS = 128
PEER_HEADS = 8
PEER_TOPK = 16
PEER_HALF = 128
PEER_QCOLS = PEER_HEADS * 2 * PEER_HALF
PEER_PICKS = PEER_HEADS * PEER_TOPK

LANES = 128
SUBLANES = 8
GATE_PAD = LANES
GLA_IN_COLS = 2 * GLA_QK + 2 * GLA_WIDTH + GATE_PAD
ATT_IN_COLS = 3 * DSW_WIDTH

VMEM_LIMIT = 56 * 1024 * 1024


def _cparams(*sem):
    return pltpu.CompilerParams(dimension_semantics=sem, vmem_limit_bytes=VMEM_LIMIT)


def _rms(x, g):
    return x * lax.rsqrt(jnp.mean(x * x, axis=-1, keepdims=True) + NORM_EPS) * g


IN_TM = 512


def _in_proj_kernel(x_ref, pos_ref, g_ref, wg_ref, wa_ref, invf_ref, sga_ref, sgb_ref,
                    gla_ref, att_ref):
    h = _rms(x_ref[...], g_ref[...]).astype(BF16)
    gla_ref[...] = jnp.dot(h, wg_ref[...], preferred_element_type=F32)
    ang = pos_ref[...].astype(F32) * invf_ref[...]
    cos = jnp.cos(ang)
    sin = jnp.sin(ang)
    sin_a = sin * sga_ref[...]
    sin_b = sin * sgb_ref[...]
    half = ROT_DIM // 2
    for c in range(ATT_IN_COLS // LANES):
        y = jnp.dot(h, wa_ref[:, c * LANES:(c + 1) * LANES], preferred_element_type=F32)
        if c < 2 * DSW_WIDTH // LANES:
            y = (y * cos + pltpu.roll(y, LANES - half, axis=1) * sin_a
                 + pltpu.roll(y, half, axis=1) * sin_b)
        att_ref[:, c * LANES:(c + 1) * LANES] = y


def _in_proj(x2, pos2, g, wg, wa, invf, sga, sgb):
    t = x2.shape[0]
    full = lambda shape: pl.BlockSpec(shape, lambda i: (0, 0))
    return pl.pallas_call(
        _in_proj_kernel,
        grid=(t // IN_TM,),
        in_specs=[pl.BlockSpec((IN_TM, D_MODEL), lambda i: (i, 0)),
                  pl.BlockSpec((IN_TM, 1), lambda i: (i, 0)),
                  full((1, D_MODEL)), full((D_MODEL, GLA_IN_COLS)), full((D_MODEL, ATT_IN_COLS)),
                  full((1, LANES)), full((1, LANES)), full((1, LANES))],
        out_specs=[pl.BlockSpec((IN_TM, GLA_IN_COLS), lambda i: (i, 0)),
                   pl.BlockSpec((IN_TM, ATT_IN_COLS), lambda i: (i, 0))],
        out_shape=[jax.ShapeDtypeStruct((t, GLA_IN_COLS), F32),
                   jax.ShapeDtypeStruct((t, ATT_IN_COLS), F32)],
        compiler_params=_cparams("parallel"),
        name="in_proj",
    )(x2, pos2, g, wg, wa, invf, sga, sgb)


GLA_SB = 512


def _log_sigmoid(z):
    return jnp.minimum(z, 0.0) - jnp.log(1.0 + jnp.exp(-jnp.abs(z)))


def _gla_kernel(in_ref, wup_ref, bup_ref, gout_ref, o_ref, state_ref):
    @pl.when(pl.program_id(1) == 0)
    def _():
        state_ref[...] = jnp.zeros_like(state_ref)

    c_i = lax.broadcasted_iota(jnp.int32, (GLA_CHUNK, GLA_CHUNK), 0)
    c_j = lax.broadcasted_iota(jnp.int32, (GLA_CHUNK, GLA_CHUNK), 1)
    causal = c_j <= c_i
    tri = causal.astype(F32)
    o_q, o_k, o_v, o_r, o_g = 0, GLA_QK, 2 * GLA_QK, 2 * GLA_QK + GLA_WIDTH, 2 * GLA_QK + 2 * GLA_WIDTH

    def chunk(c, carry):
        r0 = pl.multiple_of(c * GLA_CHUNK, GLA_CHUNK)
        rows = pl.ds(r0, GLA_CHUNK)
        code = in_ref[0, rows, o_g:o_g + GATE_PAD]
        pre = jnp.dot(code, wup_ref[...], preferred_element_type=F32) + bup_ref[...]
        log_a = _log_sigmoid(pre) * (1.0 / GLA_TAU)
        b = jnp.dot(tri, log_a, preferred_element_type=F32,
                    precision=lax.Precision.HIGHEST)
        b_last = b[GLA_CHUNK - 1:GLA_CHUNK, :]
        q = in_ref[0, rows, o_q:o_q + GLA_QK]
        k = in_ref[0, rows, o_k:o_k + GLA_QK]
        q_dec = q * (GLA_DK ** -0.5) * jnp.exp(b)
        k_dec = k * jnp.exp(-b)
        k_end = k * jnp.exp(b_last - b)
        decay = jnp.exp(b_last)
        for h in range(GLA_HEADS):
            ks = slice(h * GLA_DK, (h + 1) * GLA_DK)
            vs = slice(h * GLA_DV, (h + 1) * GLA_DV)
            v = in_ref[0, rows, o_v + h * GLA_DV:o_v + (h + 1) * GLA_DV]
            r = in_ref[0, rows, o_r + h * GLA_DV:o_r + (h + 1) * GLA_DV]
            qd = q_dec[:, ks]
            attn = lax.dot_general(qd, k_dec[:, ks], (((1,), (1,)), ((), ())),
                                   preferred_element_type=F32)
            attn = jnp.where(causal, attn, 0.0)
            st = state_ref[h]
            o = (jnp.dot(attn, v, preferred_element_type=F32)
                 + lax.dot_general(qd, st, (((1,), (1,)), ((), ())), preferred_element_type=F32))
            kv_t = lax.dot_general(v, k_end[:, ks], (((0,), (0,)), ((), ())),
                                   preferred_element_type=F32)
            state_ref[h] = st * decay[:, ks] + kv_t
            o = _rms(o, gout_ref[h:h + 1, :])
            o_ref[0, rows, vs] = o * (r / (1.0 + jnp.exp(-r)))
        return carry

    lax.fori_loop(0, GLA_SB // GLA_CHUNK, chunk, 0)


def _gla(gla_in, wup, bup, gout):
    b, s, _ = gla_in.shape
    full = lambda shape: pl.BlockSpec(shape, lambda i, j: (0,) * len(shape))
    return pl.pallas_call(
        _gla_kernel,
        grid=(b, s // GLA_SB),
        in_specs=[pl.BlockSpec((1, GLA_SB, GLA_IN_COLS), lambda i, j: (i, j, 0)),
                  full((GATE_PAD, GLA_QK)), full((1, GLA_QK)), full((GLA_HEADS, GLA_DV))],
        out_specs=pl.BlockSpec((1, GLA_SB, GLA_WIDTH), lambda i, j: (i, j, 0)),
        out_shape=jax.ShapeDtypeStruct((b, s, GLA_WIDTH), F32),
        scratch_shapes=[pltpu.VMEM((GLA_HEADS, GLA_DV, GLA_DK), F32)],
        compiler_params=_cparams("parallel", "arbitrary"),
        name="gla",
    )(gla_in, wup, bup, gout)


HEADS_PER_STEP = LANES // DSW_HEAD_DIM


def _dsw_kernel(q_ref, k_ref, v_ref, o_ref, lse_ref, *, n_back):
    m_len = q_ref.shape[1]
    nb = m_len // DSW_BLOCK
    qi = lax.broadcasted_iota(jnp.int32, (DSW_BLOCK, 2 * DSW_BLOCK), 0)
    ki = lax.broadcasted_iota(jnp.int32, (DSW_BLOCK, 2 * DSW_BLOCK), 1)
    rel = qi - ki

    def block(n, carry):
        cur_off = jnp.where(n == 0, 0, DSW_BLOCK)
        q0 = pl.multiple_of(n * DSW_BLOCK, DSW_BLOCK)
        k0 = pl.multiple_of(n * DSW_BLOCK - cur_off, DSW_BLOCK)
        dist = rel + cur_off
        mask = (dist >= 0) & (dist <= n_back)
        q = q_ref[0, pl.ds(q0, DSW_BLOCK), :]
        k = k_ref[0, pl.ds(k0, 2 * DSW_BLOCK), :]
        v = v_ref[0, pl.ds(k0, 2 * DSW_BLOCK), :]
        outs, lses = [], []
        for h in range(HEADS_PER_STEP):
            hs = slice(h * DSW_HEAD_DIM, (h + 1) * DSW_HEAD_DIM)
            s = lax.dot_general(q[:, hs].astype(BF16), k[:, hs].astype(BF16),
                                (((1,), (1,)), ((), ())), preferred_element_type=F32)
            s = jnp.where(mask, s * (DSW_HEAD_DIM ** -0.5), -jnp.inf)
            m = jnp.max(s, axis=-1, keepdims=True)
            p = jnp.exp(s - m)
            l = jnp.sum(p, axis=-1, keepdims=True)
            o = jnp.dot(p.astype(BF16), v[:, hs].astype(BF16), preferred_element_type=F32)
            outs.append(o / l)
            lses.append(jnp.broadcast_to(m + jnp.log(l), (DSW_BLOCK, DSW_HEAD_DIM)))
        o_ref[0, pl.ds(q0, DSW_BLOCK), :] = jnp.concatenate(outs, axis=-1)
        lse_ref[0, pl.ds(q0, DSW_BLOCK), :] = jnp.concatenate(lses, axis=-1)
        return carry

    lax.fori_loop(0, nb, block, 0)


def _dsw(att_in, window, dilation):
    b, s, _ = att_in.shape
    m_len = s // dilation
    n_back = window // dilation
    assert s % (dilation * DSW_BLOCK) == 0 and m_len // DSW_BLOCK >= 2
    view = att_in.reshape(b, m_len, dilation * ATT_IN_COLS)
    per_tok = ATT_IN_COLS // LANES
    n_hp = DSW_WIDTH // LANES
    blk = (1, m_len, LANES)
    in_spec = lambda off: pl.BlockSpec(blk, lambda i, r, hp: (i, 0, r * per_tok + off + hp))
    out_spec = pl.BlockSpec(blk, lambda i, r, hp: (i, 0, r * n_hp + hp))
    out_sds = jax.ShapeDtypeStruct((b, m_len, dilation * DSW_WIDTH), F32)
    o, lse = pl.pallas_call(
        functools.partial(_dsw_kernel, n_back=n_back),
        grid=(b, dilation, n_hp),
        in_specs=[in_spec(0), in_spec(n_hp), in_spec(2 * n_hp)],
        out_specs=[out_spec, out_spec],
        out_shape=[out_sds, out_sds],
        compiler_params=_cparams("parallel", "parallel", "parallel"),
        name=f"dsw_d{dilation}",
    )(view, view, view)
    return o.reshape(b, s, DSW_WIDTH), lse.reshape(b, s, DSW_WIDTH)


MIX_TM = 256
PEER_GROUPS = PEER_HEADS * 2


def _mix_out_kernel(x_ref, oa_ref, o1_ref, o2_ref, o3_ref, l1_ref, l2_ref, l3_ref,
                    wo_ref, gp_ref, wq_ref, sk_ref, x1_ref, h_ref, sc_ref):
    l1, l2, l3 = l1_ref[...], l2_ref[...], l3_ref[...]
    m = jnp.maximum(jnp.maximum(l1, l2), l3)
    e1, e2, e3 = jnp.exp(l1 - m), jnp.exp(l2 - m), jnp.exp(l3 - m)
    ob = (e1 * o1_ref[...] + e2 * o2_ref[...] + e3 * o3_ref[...]) / (e1 + e2 + e3)
    mix = (jnp.dot(oa_ref[...].astype(BF16), wo_ref[:GLA_WIDTH, :], preferred_element_type=F32)
           + jnp.dot(ob.astype(BF16), wo_ref[GLA_WIDTH:, :], preferred_element_type=F32))
    x1 = x_ref[...] + mix
    x1_ref[...] = x1
    h = _rms(x1, gp_ref[...])
    h_ref[...] = h
    hb = h.astype(BF16)
    for g in range(PEER_GROUPS):
        qg = jnp.dot(hb, wq_ref[:, g * PEER_HALF:(g + 1) * PEER_HALF], preferred_element_type=F32)
        sc_ref[g] = lax.dot_general(sk_ref[g], qg.astype(BF16), (((1,), (1,)), ((), ())),
                                    preferred_element_type=F32)


def _mix_out(x2, oa, o1, o2, o3, l1, l2, l3, wo, gp, wq, sk):
    t = x2.shape[0]
    tok = lambda w: pl.BlockSpec((MIX_TM, w), lambda i: (i, 0))
    full = lambda shape: pl.BlockSpec(shape, lambda i: (0,) * len(shape))
    return pl.pallas_call(
        _mix_out_kernel,
        grid=(t // MIX_TM,),
        in_specs=[tok(D_MODEL)] + [tok(DSW_WIDTH)] * 7
                 + [full((D_MODEL, D_MODEL)), full((1, D_MODEL)), full((D_MODEL, PEER_QCOLS)),
                    full((PEER_GROUPS, PEER_N_KEYS, PEER_HALF))],
        out_specs=[tok(D_MODEL), tok(D_MODEL),
                   pl.BlockSpec((PEER_GROUPS, PEER_N_KEYS, MIX_TM), lambda i: (0, 0, i))],
        out_shape=[jax.ShapeDtypeStruct((t, D_MODEL), F32), jax.ShapeDtypeStruct((t, D_MODEL), F32),
                   jax.ShapeDtypeStruct((PEER_GROUPS, PEER_N_KEYS, t), F32)],
        compiler_params=_cparams("parallel"),
        name="mix_out",
    )(x2, oa, o1, o2, o3, l1, l2, l3, wo, gp, wq, sk)


TOPK_TT = 128


def _top_rows(s, payload=None):
    n_rows = s.shape[0]
    rows = lax.broadcasted_iota(jnp.int32, s.shape, 0).astype(F32)
    vals, idxs, pays = [], [], []
    for _ in range(PEER_TOPK):
        m = jnp.max(s, axis=0, keepdims=True)
        i = jnp.min(jnp.where(s == m, rows, float(n_rows)), axis=0, keepdims=True)
        hit = rows == i
        if payload is not None:
            pays.append(jnp.max(jnp.where(hit, payload, -1.0), axis=0, keepdims=True))
        s = jnp.where(hit, -jnp.inf, s)
        vals.append(m)
        idxs.append(i)
    out = (jnp.concatenate(vals, axis=0), jnp.concatenate(idxs, axis=0))
    if payload is not None:
        out = out + (jnp.concatenate(pays, axis=0),)
    return out


def _peer_topk_kernel(sc_ref, idx_ref, gate_ref):
    def head(h, carry):
        v0, i0 = _top_rows(sc_ref[2 * h])
        v1, i1 = _top_rows(sc_ref[2 * h + 1])
        cand = jnp.concatenate([v0[i:i + 1, :] + v1 for i in range(PEER_TOPK)], axis=0)
        cidx = jnp.concatenate([i0[i:i + 1, :] * float(PEER_N_KEYS) + i1 for i in range(PEER_TOPK)],
                               axis=0)
        best, _, expert = _top_rows(cand, cidx)
        e = jnp.exp(best - best[0:1, :])
        gates = e / jnp.sum(e, axis=0, keepdims=True)
        r0 = pl.multiple_of(h * PEER_TOPK, PEER_TOPK)
        idx_ref[pl.ds(r0, PEER_TOPK), :] = expert.astype(jnp.int32)
        gate_ref[pl.ds(r0, PEER_TOPK), :] = gates
        return carry

    lax.fori_loop(0, PEER_HEADS, head, 0)


def _peer_topk(scores_t):
    t = scores_t.shape[-1]
    out_spec = pl.BlockSpec((PEER_PICKS, TOPK_TT), lambda i: (0, i))
    return pl.pallas_call(
        _peer_topk_kernel,
        grid=(t // TOPK_TT,),
        in_specs=[pl.BlockSpec((PEER_GROUPS, PEER_N_KEYS, TOPK_TT), lambda i: (0, 0, i))],
        out_specs=[out_spec, out_spec],
        out_shape=[jax.ShapeDtypeStruct((PEER_PICKS, t), jnp.int32),
                   jax.ShapeDtypeStruct((PEER_PICKS, t), F32)],
        compiler_params=_cparams("parallel"),
        name="peer_topk",
    )(scores_t)


FFN_TB = 64
FFN_NBUF = 8
ROW_SUB = 2 * D_MODEL // LANES
D_SUB = D_MODEL // LANES


def _sublane_fold(vs):
    sub = lax.broadcasted_iota(jnp.int32, (SUBLANES, LANES), 0)
    span = SUBLANES // 2
    while len(vs) > 1:
        low = (sub % (2 * span)) < span
        nxt = []
        for j in range(len(vs) // 2):
            a, b = vs[j], vs[j + len(vs) // 2]
            if span == SUBLANES // 2:
                nxt.append(jnp.where(low, a, b) + pltpu.roll(jnp.where(low, b, a), span, axis=0))
            else:
                nxt.append(jnp.where(low, a + pltpu.roll(a, SUBLANES - span, axis=0),
                                     b + pltpu.roll(b, span, axis=0)))
        vs = nxt
        span //= 2
    return vs[0]


def _peer_ffn_kernel(idx_ref, idx_next_ref, h_ref, gt_ref, uv_ref, o_ref, buf, act_ref, sems):
    step = pl.program_id(0)
    last = pl.num_programs(0) - 1
    ahead = FFN_NBUF - 1

    def row_copy(src_idx_ref, row, k, slot):
        return pltpu.make_async_copy(uv_ref.at[src_idx_ref[row, k]], buf.at[slot, k], sems.at[slot])

    def wait(slot):
        pltpu.make_async_copy(uv_ref.at[pl.ds(0, PEER_PICKS)], buf.at[slot], sems.at[slot]).wait()

    @pl.when(step == 0)
    def _():
        def prime(t, carry):
            for k in range(PEER_PICKS):
                row_copy(idx_ref, t, k, t).start(priority=k % 2)
            return carry

        lax.fori_loop(0, ahead, prime, 0)

    lane = lax.broadcasted_iota(jnp.int32, (PEER_PICKS, LANES), 1)
    lane0 = (step * FFN_TB) % LANES

    def token(t, src_idx_ref, src_row):
        slot = t % FFN_NBUF
        nslot = (t + ahead) % FFN_NBUF

        def issue(k):
            row_copy(src_idx_ref, src_row, k, nslot).start(priority=k % 2)

        wait(slot)
        x = h_ref[t]
        packed = []
        for g in range(PEER_PICKS // SUBLANES):
            prods = []
            for j in range(SUBLANES):
                k = g * SUBLANES + j
                if k % 2 == 0:
                    issue(k // 2)
                prods.append(buf[slot, k, 0:D_SUB, :] * x)
            packed.append(_sublane_fold(prods))
        part = jnp.concatenate(packed, axis=0)
        a = jnp.sum(part, axis=-1, keepdims=True)
        gate = jnp.sum(jnp.where(lane == lane0 + t, gt_ref[...], 0.0), axis=-1, keepdims=True)
        act = 0.5 * a * (1.0 + lax.erf(a * (2.0 ** -0.5))) * gate
        act_ref[...] = jnp.broadcast_to(act, (PEER_PICKS, LANES))
        acc = jnp.zeros((D_SUB, LANES), F32)
        for k in range(PEER_PICKS):
            if k % 2 == 0:
                issue(PEER_PICKS // 2 + k // 2)
            acc = acc + act_ref[k:k + 1, :] * buf[slot, k, D_SUB:ROW_SUB, :]
        o_ref[t] = acc

    def same_step(t, carry):
        token(t, idx_ref, t + ahead)
        return carry

    def next_step(t, carry):
        token(t, idx_next_ref, t + ahead - FFN_TB)
        return carry

    lax.fori_loop(0, FFN_TB - ahead, same_step, 0)
    lax.fori_loop(FFN_TB - ahead, FFN_TB, next_step, 0)

    @pl.when(step == last)
    def _():
        for j in range(ahead):
            wait(j)


def _peer_ffn(idx, h3, gates_t, uv):
    t = idx.shape[0]
    n_steps = t // FFN_TB
    assert FFN_TB % FFN_NBUF == 0 and LANES % FFN_TB == 0
    idx_spec = lambda index_map: pl.BlockSpec((FFN_TB, PEER_PICKS), index_map, memory_space=pltpu.SMEM)
    return pl.pallas_call(
        _peer_ffn_kernel,
        grid=(n_steps,),
        in_specs=[idx_spec(lambda i: (i, 0)),
                  idx_spec(lambda i: (jnp.minimum(i + 1, n_steps - 1), 0)),
                  pl.BlockSpec((FFN_TB, D_SUB, LANES), lambda i: (i, 0, 0)),
                  pl.BlockSpec((PEER_PICKS, LANES), lambda i: (0, i * FFN_TB // LANES)),
                  pl.BlockSpec(memory_space=pl.ANY)],
        out_specs=pl.BlockSpec((FFN_TB, D_SUB, LANES), lambda i: (i, 0, 0)),
        out_shape=jax.ShapeDtypeStruct((t, D_SUB, LANES), F32),
        scratch_shapes=[pltpu.VMEM((FFN_NBUF, PEER_PICKS, ROW_SUB, LANES), F32),
                        pltpu.VMEM((PEER_PICKS, LANES), F32),
                        pltpu.SemaphoreType.DMA((FFN_NBUF,))],
        compiler_params=_cparams("arbitrary"),
        name="peer_ffn",
    )(idx, idx, h3, gates_t, uv)


FIN_TM = 256


def _final_kernel(x1_ref, peer_ref, p_ref, gple_ref, wg_ref, wp_ref, gfin_ref, o_ref):
    x2 = x1_ref[...] + peer_ref[...]
    z = jnp.dot(_rms(x2, gple_ref[...]).astype(BF16), wg_ref[...], preferred_element_type=F32)
    gate = 1.0 / (1.0 + jnp.exp(-z))
    emb = jnp.dot(p_ref[...].astype(BF16), wp_ref[...], preferred_element_type=F32)
    o_ref[...] = _rms(x2 + emb * gate, gfin_ref[...])


def _final(x1, peer_out, p2, gple, wg, wp, gfin):
    t = x1.shape[0]
    tok = lambda w: pl.BlockSpec((FIN_TM, w), lambda i: (i, 0))
    full = lambda shape: pl.BlockSpec(shape, lambda i: (0, 0))
    return pl.pallas_call(
        _final_kernel,
        grid=(t // FIN_TM,),
        in_specs=[tok(D_MODEL), tok(D_MODEL), tok(PLE_DIM), full((1, D_MODEL)),
                  full((D_MODEL, D_MODEL)), full((PLE_DIM, D_MODEL)), full((1, D_MODEL))],
        out_specs=tok(D_MODEL),
        out_shape=jax.ShapeDtypeStruct((t, D_MODEL), F32),
        compiler_params=_cparams("parallel"),
        name="final",
    )(x1, peer_out, p2, gple, wg, wp, gfin)


def _rope_lane_constants():
    lane = np.arange(LANES) % DSW_HEAD_DIM
    half = ROT_DIM // 2
    inv_freq = ROPE_THETA ** (-jnp.arange(0, ROT_DIM, 2, dtype=F32) / ROT_DIM)
    invf = jnp.where(lane < ROT_DIM, inv_freq[lane % half], 0.0).astype(F32)[None, :]
    sga = jnp.asarray(np.where(lane < half, -1.0, 0.0), F32)[None, :]
    sgb = jnp.asarray(np.where((lane >= half) & (lane < ROT_DIM), 1.0, 0.0), F32)[None, :]
    return invf, sga, sgb


def kernel(x, p, positions, g_mix, w_in, w_gla_gate_up, b_gla_gate, g_gla_out, w_out, g_peer,
           w_peer_q, peer_sub_keys, peer_u, peer_v, g_ple, w_ple_gate, w_ple_proj, g_final):
    bsz, slen, dm = x.shape
    t = bsz * slen
    x2 = x.reshape(t, dm)
    pos2 = positions.reshape(t, 1)

    w = w_in[0]
    c_gate = 2 * GLA_QK + 2 * GLA_WIDTH
    w_gla = jnp.concatenate(
        [w[:, :c_gate], jnp.pad(w[:, c_gate:c_gate + GLA_GATE_RANK], ((0, 0), (0, GATE_PAD - GLA_GATE_RANK)))],
        axis=1).astype(BF16)
    w_att = w[:, c_gate + GLA_GATE_RANK:].astype(BF16)
    invf, sga, sgb = _rope_lane_constants()
    gla_in, att_in = _in_proj(x2, pos2, g_mix[0][None, :], w_gla, w_att, invf, sga, sgb)

    wup = jnp.pad(w_gla_gate_up[0], ((0, GATE_PAD - GLA_GATE_RANK), (0, 0)))
    oa = _gla(gla_in.reshape(bsz, slen, GLA_IN_COLS), wup, b_gla_gate[0][None, :], g_gla_out[0])

    att3 = att_in.reshape(bsz, slen, ATT_IN_COLS)
    pats = [_dsw(att3, window, dilation) for window, dilation in DSW_PATTERNS]
    flat = lambda a: a.reshape(t, a.shape[-1])

    sub_keys = peer_sub_keys[0].reshape(PEER_GROUPS, PEER_N_KEYS, PEER_HALF).astype(BF16)
    x1, h, scores_t = _mix_out(
        x2, flat(oa), flat(pats[0][0]), flat(pats[1][0]), flat(pats[2][0]),
        flat(pats[0][1]), flat(pats[1][1]), flat(pats[2][1]),
        w_out[0].astype(BF16), g_peer[0][None, :], w_peer_q[0].astype(BF16), sub_keys)

    idx_t, gates_t = _peer_topk(scores_t)

    n_exp = peer_u.shape[1]
    uv = jnp.concatenate([peer_u[0].reshape(n_exp, D_SUB, LANES),
                          peer_v[0].reshape(n_exp, D_SUB, LANES)], axis=1)
    peer_out = _peer_ffn(idx_t.T, h.reshape(t, D_SUB, LANES), gates_t, uv)

    out = _final(x1, peer_out.reshape(t, dm), p[0].reshape(t, PLE_DIM), g_ple[0][None, :],
                 w_ple_gate[0].astype(BF16), w_ple_proj[0].astype(BF16), g_final[None, :])
    return out.reshape(bsz, slen, dm)
```

```python
import functools
import math

import jax
import jax.numpy as jnp
import numpy as np
from jax import lax
from jax.experimental import pallas as pl
from jax.experimental.pallas import tpu as pltpu

F32 = jnp.float32
BF16 = jnp.bfloat16

D_MODEL = 1024
PLE_DIM = 256
NORM_EPS = 1e-6
GLA_HEADS = 4
GLA_DK = 64
GLA_DV = 128
GLA_QK = GLA_HEADS * GLA_DK
GLA_WIDTH = GLA_HEADS * GLA_DV
GLA_GATE_RANK = 16
GLA_TAU = 16.0
GLA_CHUNK = 64
DSW_HEAD_DIM = 64
DSW_HEADS = 8
DSW_WIDTH = DSW_HEADS * DSW_HEAD_DIM
DSW_PATTERNS = ((128, 1), (512, 4), (2048, 16))
DSW_BLOCK = 128
ROT_DIM = 16
ROPE_THETA = 500000.0
PEER_N_KEYS = 128
PEER_HEADS = 8
PEER_TOPK = 16
PEER_HALF = 128
PEER_QCOLS = PEER_HEADS * 2 * PEER_HALF
PEER_PICKS = PEER_HEADS * PEER_TOPK

LANES = 128
SUBLANES = 8
GATE_PAD = LANES
GLA_IN_COLS = 2 * GLA_QK + 2 * GLA_WIDTH + GATE_PAD
ATT_IN_COLS = 3 * DSW_WIDTH

VMEM_LIMIT = 56 * 1024 * 1024


def _cparams(*sem):
    return pltpu.CompilerParams(dimension_semantics=sem, vmem_limit_bytes=VMEM_LIMIT)


def _rms(x, g):
    return x * lax.rsqrt(jnp.mean(x * x, axis=-1, keepdims=True) + NORM_EPS) * g


def _transpose8(vs):
    sub = lax.broadcasted_iota(jnp.int32, (SUBLANES, LANES), 0)
    vs = list(vs)
    span = SUBLANES // 2
    while span >= 1:
        low = (sub % (2 * span)) < span
        for c in range(SUBLANES):
            if c % (2 * span) < span:
                a, b = vs[c], vs[c + span]
                vs[c] = jnp.where(low, a, pltpu.roll(b, span, axis=0))
                vs[c + span] = jnp.where(low, pltpu.roll(a, SUBLANES - span, axis=0), b)
        span //= 2
    return vs


IN_TM = 512


def _in_proj_kernel(x_ref, pos_ref, g_ref, wg_ref, wa_ref, invf_ref, sga_ref, sgb_ref,
                    gla_ref, att_ref):
    h = _rms(x_ref[...], g_ref[...]).astype(BF16)
    gla_ref[...] = jnp.dot(h, wg_ref[...], preferred_element_type=F32)
    ang = pos_ref[...].astype(F32) * invf_ref[...]
    cos = jnp.cos(ang)
    sin = jnp.sin(ang)
    sin_a = sin * sga_ref[...]
    sin_b = sin * sgb_ref[...]
    half = ROT_DIM // 2
    for c in range(ATT_IN_COLS // LANES):
        y = jnp.dot(h, wa_ref[:, c * LANES:(c + 1) * LANES], preferred_element_type=F32)
        if c < 2 * DSW_WIDTH // LANES:
            y = (y * cos + pltpu.roll(y, LANES - half, axis=1) * sin_a
                 + pltpu.roll(y, half, axis=1) * sin_b)
        att_ref[:, c * LANES:(c + 1) * LANES] = y


def _in_proj(x2, pos2, g, wg, wa, invf, sga, sgb):
    t = x2.shape[0]
    full = lambda shape: pl.BlockSpec(shape, lambda i: (0, 0))
    return pl.pallas_call(
        _in_proj_kernel,
        grid=(t // IN_TM,),
        in_specs=[pl.BlockSpec((IN_TM, D_MODEL), lambda i: (i, 0)),
                  pl.BlockSpec((IN_TM, 1), lambda i: (i, 0)),
                  full((1, D_MODEL)), full((D_MODEL, GLA_IN_COLS)), full((D_MODEL, ATT_IN_COLS)),
                  full((1, LANES)), full((1, LANES)), full((1, LANES))],
        out_specs=[pl.BlockSpec((IN_TM, GLA_IN_COLS), lambda i: (i, 0)),
                   pl.BlockSpec((IN_TM, ATT_IN_COLS), lambda i: (i, 0))],
        out_shape=[jax.ShapeDtypeStruct((t, GLA_IN_COLS), F32),
                   jax.ShapeDtypeStruct((t, ATT_IN_COLS), F32)],
        compiler_params=_cparams("parallel"),
        name="in_proj",
    )(x2, pos2, g, wg, wa, invf, sga, sgb)


GLA_SB = 512


def _log_sigmoid(z):
    return jnp.minimum(z, 0.0) - jnp.log(1.0 + jnp.exp(-jnp.abs(z)))


def _gla_kernel(in_ref, wup_ref, bup_ref, gout_ref, o_ref, state_ref):
    @pl.when(pl.program_id(1) == 0)
    def _():
        state_ref[...] = jnp.zeros_like(state_ref)

    c_i = lax.broadcasted_iota(jnp.int32, (GLA_CHUNK, GLA_CHUNK), 0)
    c_j = lax.broadcasted_iota(jnp.int32, (GLA_CHUNK, GLA_CHUNK), 1)
    causal = c_j <= c_i
    tri = causal.astype(F32)
    o_q, o_k, o_v, o_r, o_g = 0, GLA_QK, 2 * GLA_QK, 2 * GLA_QK + GLA_WIDTH, 2 * GLA_QK + 2 * GLA_WIDTH

    def chunk(c, carry):
        r0 = pl.multiple_of(c * GLA_CHUNK, GLA_CHUNK)
        rows = pl.ds(r0, GLA_CHUNK)
        code = in_ref[0, rows, o_g:o_g + GATE_PAD]
        pre = jnp.dot(code, wup_ref[...], preferred_element_type=F32) + bup_ref[...]
        log_a = _log_sigmoid(pre) * (1.0 / GLA_TAU)
        b = jnp.dot(tri, log_a, preferred_element_type=F32,
                    precision=lax.Precision.HIGHEST)
        b_last = b[GLA_CHUNK - 1:GLA_CHUNK, :]
        q = in_ref[0, rows, o_q:o_q + GLA_QK]
        k = in_ref[0, rows, o_k:o_k + GLA_QK]
        q_dec = q * (GLA_DK ** -0.5) * jnp.exp(b)
        k_dec = k * jnp.exp(-b)
        k_end = k * jnp.exp(b_last - b)
        decay = jnp.exp(b_last)
        for h in range(GLA_HEADS):
            ks = slice(h * GLA_DK, (h + 1) * GLA_DK)
            vs = slice(h * GLA_DV, (h + 1) * GLA_DV)
            v = in_ref[0, rows, o_v + h * GLA_DV:o_v + (h + 1) * GLA_DV]
            r = in_ref[0, rows, o_r + h * GLA_DV:o_r + (h + 1) * GLA_DV]
            qd = q_dec[:, ks]
            attn = lax.dot_general(qd, k_dec[:, ks], (((1,), (1,)), ((), ())),
                                   preferred_element_type=F32)
            attn = jnp.where(causal, attn, 0.0)
            st = state_ref[h]
            o = (jnp.dot(attn, v, preferred_element_type=F32)
                 + lax.dot_general(qd, st, (((1,), (1,)), ((), ())), preferred_element_type=F32))
            kv_t = lax.dot_general(v, k_end[:, ks], (((0,), (0,)), ((), ())),
                                   preferred_element_type=F32)
            state_ref[h] = st * decay[:, ks] + kv_t
            o = _rms(o, gout_ref[h:h + 1, :])
            o_ref[0, rows, vs] = o * (r / (1.0 + jnp.exp(-r)))
        return carry

    lax.fori_loop(0, GLA_SB // GLA_CHUNK, chunk, 0)


def _gla(gla_in, wup, bup, gout):
    b, s, _ = gla_in.shape
    full = lambda shape: pl.BlockSpec(shape, lambda i, j: (0,) * len(shape))
    return pl.pallas_call(
        _gla_kernel,
        grid=(b, s // GLA_SB),
        in_specs=[pl.BlockSpec((1, GLA_SB, GLA_IN_COLS), lambda i, j: (i, j, 0)),
                  full((GATE_PAD, GLA_QK)), full((1, GLA_QK)), full((GLA_HEADS, GLA_DV))],
        out_specs=pl.BlockSpec((1, GLA_SB, GLA_WIDTH), lambda i, j: (i, j, 0)),
        out_shape=jax.ShapeDtypeStruct((b, s, GLA_WIDTH), F32),
        scratch_shapes=[pltpu.VMEM((GLA_HEADS, GLA_DV, GLA_DK), F32)],
        compiler_params=_cparams("parallel", "arbitrary"),
        name="gla",
    )(gla_in, wup, bup, gout)


HEADS_PER_STEP = LANES // DSW_HEAD_DIM


def _dsw_kernel(q_ref, k_ref, v_ref, o_ref, lse_ref, *, n_back):
    m_len = q_ref.shape[1]
    nb = m_len // DSW_BLOCK
    qi = lax.broadcasted_iota(jnp.int32, (DSW_BLOCK, 2 * DSW_BLOCK), 0)
    ki = lax.broadcasted_iota(jnp.int32, (DSW_BLOCK, 2 * DSW_BLOCK), 1)
    rel = qi - ki

    def block(n, carry):
        cur_off = jnp.where(n == 0, 0, DSW_BLOCK)
        q0 = pl.multiple_of(n * DSW_BLOCK, DSW_BLOCK)
        k0 = pl.multiple_of(n * DSW_BLOCK - cur_off, DSW_BLOCK)
        dist = rel + cur_off
        mask = (dist >= 0) & (dist <= n_back)
        q = q_ref[0, pl.ds(q0, DSW_BLOCK), :]
        k = k_ref[0, pl.ds(k0, 2 * DSW_BLOCK), :]
        v = v_ref[0, pl.ds(k0, 2 * DSW_BLOCK), :]
        outs, lses = [], []
        for h in range(HEADS_PER_STEP):
            hs = slice(h * DSW_HEAD_DIM, (h + 1) * DSW_HEAD_DIM)
            s = lax.dot_general(q[:, hs].astype(BF16), k[:, hs].astype(BF16),
                                (((1,), (1,)), ((), ())), preferred_element_type=F32)
            s = jnp.where(mask, s * (DSW_HEAD_DIM ** -0.5), -jnp.inf)
            m = jnp.max(s, axis=-1, keepdims=True)
            p = jnp.exp(s - m)
            l = jnp.sum(p, axis=-1, keepdims=True)
            o = jnp.dot(p.astype(BF16), v[:, hs].astype(BF16), preferred_element_type=F32)
            outs.append(o / l)
            lses.append(jnp.broadcast_to(m + jnp.log(l), (DSW_BLOCK, DSW_HEAD_DIM)))
        o_ref[0, pl.ds(q0, DSW_BLOCK), :] = jnp.concatenate(outs, axis=-1)
        lse_ref[0, pl.ds(q0, DSW_BLOCK), :] = jnp.concatenate(lses, axis=-1)
        return carry

    lax.fori_loop(0, nb, block, 0)


def _dsw(att_in, window, dilation):
    b, s, _ = att_in.shape
    m_len = s // dilation
    n_back = window // dilation
    assert s % (dilation * DSW_BLOCK) == 0 and m_len // DSW_BLOCK >= 2
    view = att_in.reshape(b, m_len, dilation * ATT_IN_COLS)
    per_tok = ATT_IN_COLS // LANES
    n_hp = DSW_WIDTH // LANES
    blk = (1, m_len, LANES)
    in_spec = lambda off: pl.BlockSpec(blk, lambda i, r, hp: (i, 0, r * per_tok + off + hp))
    out_spec = pl.BlockSpec(blk, lambda i, r, hp: (i, 0, r * n_hp + hp))
    out_sds = jax.ShapeDtypeStruct((b, m_len, dilation * DSW_WIDTH), F32)
    o, lse = pl.pallas_call(
        functools.partial(_dsw_kernel, n_back=n_back),
        grid=(b, dilation, n_hp),
        in_specs=[in_spec(0), in_spec(n_hp), in_spec(2 * n_hp)],
        out_specs=[out_spec, out_spec],
        out_shape=[out_sds, out_sds],
        compiler_params=_cparams("parallel", "parallel", "parallel"),
        name=f"dsw_d{dilation}",
    )(view, view, view)
    return o.reshape(b, s, DSW_WIDTH), lse.reshape(b, s, DSW_WIDTH)


MIX_TM = 256
PEER_GROUPS = PEER_HEADS * 2


def _mix_out_kernel(x_ref, oa_ref, o1_ref, o2_ref, o3_ref, l1_ref, l2_ref, l3_ref,
                    wo_ref, gp_ref, wq_ref, sk_ref, x1_ref, h_ref, sc_ref):
    l1, l2, l3 = l1_ref[...], l2_ref[...], l3_ref[...]
    m = jnp.maximum(jnp.maximum(l1, l2), l3)
    e1, e2, e3 = jnp.exp(l1 - m), jnp.exp(l2 - m), jnp.exp(l3 - m)
    ob = (e1 * o1_ref[...] + e2 * o2_ref[...] + e3 * o3_ref[...]) / (e1 + e2 + e3)
    mix = (jnp.dot(oa_ref[...].astype(BF16), wo_ref[:GLA_WIDTH, :], preferred_element_type=F32)
           + jnp.dot(ob.astype(BF16), wo_ref[GLA_WIDTH:, :], preferred_element_type=F32))
    x1 = x_ref[...] + mix
    x1_ref[...] = x1
    h = _rms(x1, gp_ref[...])
    for g in range(MIX_TM // SUBLANES):
        rows = slice(g * SUBLANES, (g + 1) * SUBLANES)
        slabs = _transpose8([h[rows, c * LANES:(c + 1) * LANES] for c in range(D_SUB)])
        for j in range(SUBLANES):
            h_ref[g * SUBLANES + j] = slabs[j]
    hb = h.astype(BF16)
    for g in range(PEER_GROUPS):
        qg = jnp.dot(hb, wq_ref[:, g * PEER_HALF:(g + 1) * PEER_HALF], preferred_element_type=F32)
        sc_ref[g] = lax.dot_general(sk_ref[g], qg.astype(BF16), (((1,), (1,)), ((), ())),
                                    preferred_element_type=F32)


def _mix_out(x2, oa, o1, o2, o3, l1, l2, l3, wo, gp, wq, sk):
    t = x2.shape[0]
    tok = lambda w: pl.BlockSpec((MIX_TM, w), lambda i: (i, 0))
    full = lambda shape: pl.BlockSpec(shape, lambda i: (0,) * len(shape))
    return pl.pallas_call(
        _mix_out_kernel,
        grid=(t // MIX_TM,),
        in_specs=[tok(D_MODEL)] + [tok(DSW_WIDTH)] * 7
                 + [full((D_MODEL, D_MODEL)), full((1, D_MODEL)), full((D_MODEL, PEER_QCOLS)),
                    full((PEER_GROUPS, PEER_N_KEYS, PEER_HALF))],
        out_specs=[tok(D_MODEL), pl.BlockSpec((MIX_TM, D_SUB, LANES), lambda i: (i, 0, 0)),
                   pl.BlockSpec((PEER_GROUPS, PEER_N_KEYS, MIX_TM), lambda i: (0, 0, i))],
        out_shape=[jax.ShapeDtypeStruct((t, D_MODEL), F32),
                   jax.ShapeDtypeStruct((t, D_SUB, LANES), F32),
                   jax.ShapeDtypeStruct((PEER_GROUPS, PEER_N_KEYS, t), F32)],
        compiler_params=_cparams("parallel"),
        name="mix_out",
    )(x2, oa, o1, o2, o3, l1, l2, l3, wo, gp, wq, sk)


TOPK_TT = 128


def _top_rows(s, order, payload=None):
    vals, idxs, pays = [], [], []
    for _ in range(PEER_TOPK):
        m = jnp.max(s, axis=0, keepdims=True)
        i = jnp.min(jnp.where(s == m, order, jnp.inf), axis=0, keepdims=True)
        hit = order == i
        if payload is not None:
            pays.append(jnp.max(jnp.where(hit, payload, -1.0), axis=0, keepdims=True))
        s = jnp.where(hit, -jnp.inf, s)
        vals.append(m)
        idxs.append(i)
    out = (jnp.concatenate(vals, axis=0), jnp.concatenate(idxs, axis=0))
    if payload is not None:
        out = out + (jnp.concatenate(pays, axis=0),)
    return out


def _pair_candidates(v0, v1, e0, e1):
    n = PEER_TOPK
    half = SUBLANES // 2
    sub = lax.broadcasted_iota(jnp.int32, (SUBLANES, v0.shape[1]), 0)
    subf = sub.astype(F32)
    lo = sub < half
    sums, flats, experts = [], [], []
    for j in range(5):
        sums.append(v0[0:SUBLANES] + v1[j:j + 1])
        experts.append(e0[0:SUBLANES] + e1[j:j + 1])
        flats.append(subf * n + j)
    sums.append(v0[SUBLANES:n] + v1[0:1])
    experts.append(e0[SUBLANES:n] + e1[0:1])
    flats.append((subf + SUBLANES) * n)
    sums.append(v0[0:1] + v1[SUBLANES:n])
    experts.append(e0[0:1] + e1[SUBLANES:n])
    flats.append(subf + SUBLANES)
    shifted = lambda a: jnp.where(lo, pltpu.roll(a[0:SUBLANES], 3, axis=0), pltpu.roll(a[0:SUBLANES], 7, axis=0))
    unused = (sub % half) == half - 1
    sums.append(jnp.where(unused, -jnp.inf, jnp.where(lo, v0[0:1], v0[1:2]) + shifted(v1)))
    experts.append(jnp.where(lo, e0[0:1], e0[1:2]) + shifted(e1))
    flats.append(jnp.where(unused, 1e9, jnp.where(lo, subf + 5, subf + (n + 1))))
    cat = lambda xs: jnp.concatenate(xs, axis=0)
    return cat(sums), cat(flats), cat(experts)


def _peer_topk_kernel(sc_ref, idx_ref, gate_ref):
    rows = lax.broadcasted_iota(jnp.int32, (PEER_N_KEYS, TOPK_TT), 0).astype(F32)

    def head(h, carry):
        v0, i0 = _top_rows(sc_ref[2 * h], rows)
        v1, i1 = _top_rows(sc_ref[2 * h + 1], rows)
        cand, flat, cexp = _pair_candidates(v0, v1, i0 * float(PEER_N_KEYS), i1)
        best, _, expert = _top_rows(cand, flat, cexp)
        e = jnp.exp(best - best[0:1, :])
        gates = e / jnp.sum(e, axis=0, keepdims=True)
        r0 = pl.multiple_of(h * PEER_TOPK, PEER_TOPK)
        idx_ref[pl.ds(r0, PEER_TOPK), :] = expert.astype(jnp.int32)
        gate_ref[pl.ds(r0, PEER_TOPK), :] = gates
        return carry

    lax.fori_loop(0, PEER_HEADS, head, 0)


def _peer_topk(scores_t):
    t = scores_t.shape[-1]
    out_spec = pl.BlockSpec((PEER_PICKS, TOPK_TT), lambda i: (0, i))
    return pl.pallas_call(
        _peer_topk_kernel,
        grid=(t // TOPK_TT,),
        in_specs=[pl.BlockSpec((PEER_GROUPS, PEER_N_KEYS, TOPK_TT), lambda i: (0, 0, i))],
        out_specs=[out_spec, out_spec],
        out_shape=[jax.ShapeDtypeStruct((PEER_PICKS, t), jnp.int32),
                   jax.ShapeDtypeStruct((PEER_PICKS, t), F32)],
        compiler_params=_cparams("parallel"),
        name="peer_topk",
    )(scores_t)


FFN_TB = 64
FFN_NBUF = 8
ROW_SUB = 2 * D_MODEL // LANES
D_SUB = D_MODEL // LANES


def _sublane_fold(vs):
    sub = lax.broadcasted_iota(jnp.int32, (SUBLANES, LANES), 0)
    span = SUBLANES // 2
    while len(vs) > 1:
        low = (sub % (2 * span)) < span
        nxt = []
        for j in range(len(vs) // 2):
            a, b = vs[j], vs[j + len(vs) // 2]
            if span == SUBLANES // 2:
                nxt.append(jnp.where(low, a, b) + pltpu.roll(jnp.where(low, b, a), span, axis=0))
            else:
                nxt.append(jnp.where(low, a + pltpu.roll(a, SUBLANES - span, axis=0),
                                     b + pltpu.roll(b, span, axis=0)))
        vs = nxt
        span //= 2
    return vs[0]


def _peer_ffn_kernel(idx_ref, idx_next_ref, h_ref, gt_ref, uv_ref, o_ref, buf, act_ref, sems):
    step = pl.program_id(0)
    last = pl.num_programs(0) - 1
    ahead = FFN_NBUF - 1

    def row_copy(src_idx_ref, row, k, slot):
        return pltpu.make_async_copy(uv_ref.at[src_idx_ref[row, k]], buf.at[slot, k], sems.at[slot])

    def wait(slot):
        pltpu.make_async_copy(uv_ref.at[pl.ds(0, PEER_PICKS)], buf.at[slot], sems.at[slot]).wait()

    @pl.when(step == 0)
    def _():
        def prime(t, carry):
            for k in range(PEER_PICKS):
                row_copy(idx_ref, t, k, t).start(priority=k % 2)
            return carry

        lax.fori_loop(0, ahead, prime, 0)

    lane = lax.broadcasted_iota(jnp.int32, (PEER_PICKS, LANES), 1)
    lane0 = (step * FFN_TB) % LANES

    def token(t, src_idx_ref, src_row):
        slot = t % FFN_NBUF
        nslot = (t + ahead) % FFN_NBUF

        def issue(k):
            row_copy(src_idx_ref, src_row, k, nslot).start(priority=k % 2)

        wait(slot)
        x = h_ref[t]
        packed = []
        for g in range(PEER_PICKS // SUBLANES):
            prods = []
            for j in range(SUBLANES):
                k = g * SUBLANES + j
                if k % 2 == 0:
                    issue(k // 2)
                prods.append(buf[slot, k, 0:D_SUB, :] * x)
            packed.append(_sublane_fold(prods))
        part = jnp.concatenate(packed, axis=0)
        a = jnp.sum(part, axis=-1, keepdims=True)
        gate = jnp.sum(jnp.where(lane == lane0 + t, gt_ref[...], 0.0), axis=-1, keepdims=True)
        act = 0.5 * a * (1.0 + lax.erf(a * (2.0 ** -0.5))) * gate
        act_ref[...] = jnp.broadcast_to(act, (PEER_PICKS, LANES))
        acc = jnp.zeros((D_SUB, LANES), F32)
        for k in range(PEER_PICKS):
            if k % 2 == 0:
                issue(PEER_PICKS // 2 + k // 2)
            acc = acc + act_ref[k:k + 1, :] * buf[slot, k, D_SUB:ROW_SUB, :]
        o_ref[t] = acc

    def same_step(t, carry):
        token(t, idx_ref, t + ahead)
        return carry

    def next_step(t, carry):
        token(t, idx_next_ref, t + ahead - FFN_TB)
        return carry

    lax.fori_loop(0, FFN_TB - ahead, same_step, 0)
    lax.fori_loop(FFN_TB - ahead, FFN_TB, next_step, 0)

    @pl.when(step == last)
    def _():
        for j in range(ahead):
            wait(j)


def _peer_ffn(idx, h3, gates_t, uv):
    t = idx.shape[0]
    n_steps = t // FFN_TB
    assert FFN_TB % FFN_NBUF == 0 and LANES % FFN_TB == 0
    idx_spec = lambda index_map: pl.BlockSpec((FFN_TB, PEER_PICKS), index_map, memory_space=pltpu.SMEM)
    return pl.pallas_call(
        _peer_ffn_kernel,
        grid=(n_steps,),
        in_specs=[idx_spec(lambda i: (i, 0)),
                  idx_spec(lambda i: (jnp.minimum(i + 1, n_steps - 1), 0)),
                  pl.BlockSpec((FFN_TB, D_SUB, LANES), lambda i: (i, 0, 0)),
                  pl.BlockSpec((PEER_PICKS, LANES), lambda i: (0, i * FFN_TB // LANES)),
                  pl.BlockSpec(memory_space=pl.ANY)],
        out_specs=pl.BlockSpec((FFN_TB, D_SUB, LANES), lambda i: (i, 0, 0)),
        out_shape=jax.ShapeDtypeStruct((t, D_SUB, LANES), F32),
        scratch_shapes=[pltpu.VMEM((FFN_NBUF, PEER_PICKS, ROW_SUB, LANES), F32),
                        pltpu.VMEM((PEER_PICKS, LANES), F32),
                        pltpu.SemaphoreType.DMA((FFN_NBUF,))],
        compiler_params=_cparams("arbitrary"),
        name="peer_ffn",
    )(idx, idx, h3, gates_t, uv)


FIN_TM = 256


def _final_kernel(x1_ref, peer_ref, p_ref, gple_ref, wg_ref, wp_ref, gfin_ref, o_ref):
    groups = []
    for g in range(FIN_TM // SUBLANES):
        cols = _transpose8([peer_ref[g * SUBLANES + j] for j in range(SUBLANES)])
        groups.append(jnp.concatenate(cols, axis=1))
    x2 = x1_ref[...] + jnp.concatenate(groups, axis=0)
    z = jnp.dot(_rms(x2, gple_ref[...]).astype(BF16), wg_ref[...], preferred_element_type=F32)
    gate = 1.0 / (1.0 + jnp.exp(-z))
    emb = jnp.dot(p_ref[...].astype(BF16), wp_ref[...], preferred_element_type=F32)
    o_ref[...] = _rms(x2 + emb * gate, gfin_ref[...])


def _final(x1, peer_out, p2, gple, wg, wp, gfin):
    t = x1.shape[0]
    tok = lambda w: pl.BlockSpec((FIN_TM, w), lambda i: (i, 0))
    full = lambda shape: pl.BlockSpec(shape, lambda i: (0, 0))
    return pl.pallas_call(
        _final_kernel,
        grid=(t // FIN_TM,),
        in_specs=[tok(D_MODEL), pl.BlockSpec((FIN_TM, D_SUB, LANES), lambda i: (i, 0, 0)),
                  tok(PLE_DIM), full((1, D_MODEL)),
                  full((D_MODEL, D_MODEL)), full((PLE_DIM, D_MODEL)), full((1, D_MODEL))],
        out_specs=tok(D_MODEL),
        out_shape=jax.ShapeDtypeStruct((t, D_MODEL), F32),
        compiler_params=_cparams("parallel"),
        name="final",
    )(x1, peer_out, p2, gple, wg, wp, gfin)


def _rope_lane_constants():
    lane = np.arange(LANES) % DSW_HEAD_DIM
    half = ROT_DIM // 2
    inv_freq = ROPE_THETA ** (-jnp.arange(0, ROT_DIM, 2, dtype=F32) / ROT_DIM)
    invf = jnp.where(lane < ROT_DIM, inv_freq[lane % half], 0.0).astype(F32)[None, :]
    sga = jnp.asarray(np.where(lane < half, -1.0, 0.0), F32)[None, :]
    sgb = jnp.asarray(np.where((lane >= half) & (lane < ROT_DIM), 1.0, 0.0), F32)[None, :]
    return invf, sga, sgb


def kernel(x, p, positions, g_mix, w_in, w_gla_gate_up, b_gla_gate, g_gla_out, w_out, g_peer,
           w_peer_q, peer_sub_keys, peer_u, peer_v, g_ple, w_ple_gate, w_ple_proj, g_final):
    bsz, slen, dm = x.shape
    t = bsz * slen
    x2 = x.reshape(t, dm)
    pos2 = positions.reshape(t, 1)

    w = w_in[0]
    c_gate = 2 * GLA_QK + 2 * GLA_WIDTH
    w_gla = jnp.concatenate(
        [w[:, :c_gate], jnp.pad(w[:, c_gate:c_gate + GLA_GATE_RANK], ((0, 0), (0, GATE_PAD - GLA_GATE_RANK)))],
        axis=1).astype(BF16)
    w_att = w[:, c_gate + GLA_GATE_RANK:].astype(BF16)
    invf, sga, sgb = _rope_lane_constants()
    gla_in, att_in = _in_proj(x2, pos2, g_mix[0][None, :], w_gla, w_att, invf, sga, sgb)

    wup = jnp.pad(w_gla_gate_up[0], ((0, GATE_PAD - GLA_GATE_RANK), (0, 0)))
    oa = _gla(gla_in.reshape(bsz, slen, GLA_IN_COLS), wup, b_gla_gate[0][None, :], g_gla_out[0])

    att3 = att_in.reshape(bsz, slen, ATT_IN_COLS)
    pats = [_dsw(att3, window, dilation) for window, dilation in DSW_PATTERNS]
    flat = lambda a: a.reshape(t, a.shape[-1])

    sub_keys = peer_sub_keys[0].reshape(PEER_GROUPS, PEER_N_KEYS, PEER_HALF).astype(BF16)
    x1, h, scores_t = _mix_out(
        x2, flat(oa), flat(pats[0][0]), flat(pats[1][0]), flat(pats[2][0]),
        flat(pats[0][1]), flat(pats[1][1]), flat(pats[2][1]),
        w_out[0].astype(BF16), g_peer[0][None, :], w_peer_q[0].astype(BF16), sub_keys)

    idx_t, gates_t = _peer_topk(scores_t)

    n_exp = peer_u.shape[1]
    uv = jnp.concatenate([peer_u[0].reshape(n_exp, D_SUB, LANES),
                          peer_v[0].reshape(n_exp, D_SUB, LANES)], axis=1)
    peer_out = _peer_ffn(idx_t.T, h, gates_t, uv)

    out = _final(x1, peer_out, p[0].reshape(t, PLE_DIM), g_ple[0][None, :],
                 w_ple_gate[0].astype(BF16), w_ple_proj[0].astype(BF16), g_final[None, :])
    return out.reshape(bsz, slen, dm)
```

```python
import functools
import math

import jax
import jax.numpy as jnp
import numpy as np
from jax import lax
from jax.experimental import pallas as pl
from jax.experimental.pallas import tpu as pltpu

F32 = jnp.float32
BF16 = jnp.bfloat16

D_MODEL = 1024
PLE_DIM = 256
NORM_EPS = 1e-6
GLA_HEADS = 4
GLA_DK = 64
GLA_DV = 128
GLA_QK = GLA_HEADS * GLA_DK
GLA_WIDTH = GLA_HEADS * GLA_DV
GLA_GATE_RANK = 16
GLA_TAU = 16.0
GLA_CHUNK = 64
DSW_HEAD_DIM = 64
DSW_HEADS = 8
DSW_WIDTH = DSW_HEADS * DSW_HEAD_DIM
DSW_PATTERNS = ((128, 1), (512, 4), (2048, 16))
DSW_BLOCK = 128
ROT_DIM = 16
ROPE_THETA = 500000.0
PEER_N_KEYS = 128
PEER_HEADS = 8
PEER_TOPK = 16
PEER_HALF = 128
PEER_QCOLS = PEER_HEADS * 2 * PEER_HALF
PEER_PICKS = PEER_HEADS * PEER_TOPK

LANES = 128
SUBLANES = 8
GATE_PAD = LANES
GLA_IN_COLS = 2 * GLA_QK + 2 * GLA_WIDTH + GATE_PAD
ATT_IN_COLS = 3 * DSW_WIDTH

VMEM_LIMIT = 56 * 1024 * 1024


def _cparams(*sem):
    return pltpu.CompilerParams(dimension_semantics=sem, vmem_limit_bytes=VMEM_LIMIT)


def _rms(x, g):
    return x * lax.rsqrt(jnp.mean(x * x, axis=-1, keepdims=True) + NORM_EPS) * g


def _transpose8(vs):
    sub = lax.broadcasted_iota(jnp.int32, (SUBLANES, LANES), 0)
    vs = list(vs)
    span = SUBLANES // 2
    while span >= 1:
        low = (sub % (2 * span)) < span
        for c in range(SUBLANES):
            if c % (2 * span) < span:
                a, b = vs[c], vs[c + span]
                vs[c] = jnp.where(low, a, pltpu.roll(b, span, axis=0))
                vs[c + span] = jnp.where(low, pltpu.roll(a, SUBLANES - span, axis=0), b)
        span //= 2
    return vs


IN_TM = 512


def _in_proj_kernel(x_ref, pos_ref, g_ref, wg_ref, wa_ref, invf_ref, sga_ref, sgb_ref,
                    gla_ref, att_ref):
    h = _rms(x_ref[...], g_ref[...]).astype(BF16)
    gla_ref[...] = jnp.dot(h, wg_ref[...], preferred_element_type=F32)
    ang = pos_ref[...].astype(F32) * invf_ref[...]
    cos = jnp.cos(ang)
    sin = jnp.sin(ang)
    sin_a = sin * sga_ref[...]
    sin_b = sin * sgb_ref[...]
    half = ROT_DIM // 2
    for c in range(ATT_IN_COLS // LANES):
        y = jnp.dot(h, wa_ref[:, c * LANES:(c + 1) * LANES], preferred_element_type=F32)
        if c < 2 * DSW_WIDTH // LANES:
            y = (y * cos + pltpu.roll(y, LANES - half, axis=1) * sin_a
                 + pltpu.roll(y, half, axis=1) * sin_b)
        att_ref[:, c * LANES:(c + 1) * LANES] = y


def _in_proj(x2, pos2, g, wg, wa, invf, sga, sgb):
    t = x2.shape[0]
    full = lambda shape: pl.BlockSpec(shape, lambda i: (0, 0))
    return pl.pallas_call(
        _in_proj_kernel,
        grid=(t // IN_TM,),
        in_specs=[pl.BlockSpec((IN_TM, D_MODEL), lambda i: (i, 0)),
                  pl.BlockSpec((IN_TM, 1), lambda i: (i, 0)),
                  full((1, D_MODEL)), full((D_MODEL, GLA_IN_COLS)), full((D_MODEL, ATT_IN_COLS)),
                  full((1, LANES)), full((1, LANES)), full((1, LANES))],
        out_specs=[pl.BlockSpec((IN_TM, GLA_IN_COLS), lambda i: (i, 0)),
                   pl.BlockSpec((IN_TM, ATT_IN_COLS), lambda i: (i, 0))],
        out_shape=[jax.ShapeDtypeStruct((t, GLA_IN_COLS), F32),
                   jax.ShapeDtypeStruct((t, ATT_IN_COLS), F32)],
        compiler_params=_cparams("parallel"),
        name="in_proj",
    )(x2, pos2, g, wg, wa, invf, sga, sgb)


GLA_SB = 512


def _log_sigmoid(z):
    return jnp.minimum(z, 0.0) - jnp.log(1.0 + jnp.exp(-jnp.abs(z)))


def _gla_kernel(in_ref, wup_ref, bup_ref, gout_ref, o_ref, state_ref):
    @pl.when(pl.program_id(1) == 0)
    def _():
        state_ref[...] = jnp.zeros_like(state_ref)

    c_i = lax.broadcasted_iota(jnp.int32, (GLA_CHUNK, GLA_CHUNK), 0)
    c_j = lax.broadcasted_iota(jnp.int32, (GLA_CHUNK, GLA_CHUNK), 1)
    causal = c_j <= c_i
    tri = causal.astype(F32)
    o_q, o_k, o_v, o_r, o_g = 0, GLA_QK, 2 * GLA_QK, 2 * GLA_QK + GLA_WIDTH, 2 * GLA_QK + 2 * GLA_WIDTH

    def chunk(c, carry):
        r0 = pl.multiple_of(c * GLA_CHUNK, GLA_CHUNK)
        rows = pl.ds(r0, GLA_CHUNK)
        code = in_ref[0, rows, o_g:o_g + GATE_PAD]
        pre = jnp.dot(code, wup_ref[...], preferred_element_type=F32) + bup_ref[...]
        log_a = _log_sigmoid(pre) * (1.0 / GLA_TAU)
        b = jnp.dot(tri, log_a, preferred_element_type=F32,
                    precision=lax.Precision.HIGHEST)
        b_last = b[GLA_CHUNK - 1:GLA_CHUNK, :]
        q = in_ref[0, rows, o_q:o_q + GLA_QK]
        k = in_ref[0, rows, o_k:o_k + GLA_QK]
        q_dec = q * (GLA_DK ** -0.5) * jnp.exp(b)
        k_dec = k * jnp.exp(-b)
        k_end = k * jnp.exp(b_last - b)
        decay = jnp.exp(b_last)
        for h in range(GLA_HEADS):
            ks = slice(h * GLA_DK, (h + 1) * GLA_DK)
            vs = slice(h * GLA_DV, (h + 1) * GLA_DV)
            v = in_ref[0, rows, o_v + h * GLA_DV:o_v + (h + 1) * GLA_DV]
            r = in_ref[0, rows, o_r + h * GLA_DV:o_r + (h + 1) * GLA_DV]
            qd = q_dec[:, ks]
            attn = lax.dot_general(qd, k_dec[:, ks], (((1,), (1,)), ((), ())),
                                   preferred_element_type=F32)
            attn = jnp.where(causal, attn, 0.0)
            st = state_ref[h]
            o = (jnp.dot(attn, v, preferred_element_type=F32)
                 + lax.dot_general(qd, st, (((1,), (1,)), ((), ())), preferred_element_type=F32))
            kv_t = lax.dot_general(v, k_end[:, ks], (((0,), (0,)), ((), ())),
                                   preferred_element_type=F32)
            state_ref[h] = st * decay[:, ks] + kv_t
            o = _rms(o, gout_ref[h:h + 1, :])
            o_ref[0, rows, vs] = o * (r / (1.0 + jnp.exp(-r)))
        return carry

    lax.fori_loop(0, GLA_SB // GLA_CHUNK, chunk, 0)


def _gla(gla_in, wup, bup, gout):
    b, s, _ = gla_in.shape
    full = lambda shape: pl.BlockSpec(shape, lambda i, j: (0,) * len(shape))
    return pl.pallas_call(
        _gla_kernel,
        grid=(b, s // GLA_SB),
        in_specs=[pl.BlockSpec((1, GLA_SB, GLA_IN_COLS), lambda i, j: (i, j, 0)),
                  full((GATE_PAD, GLA_QK)), full((1, GLA_QK)), full((GLA_HEADS, GLA_DV))],
        out_specs=pl.BlockSpec((1, GLA_SB, GLA_WIDTH), lambda i, j: (i, j, 0)),
        out_shape=jax.ShapeDtypeStruct((b, s, GLA_WIDTH), F32),
        scratch_shapes=[pltpu.VMEM((GLA_HEADS, GLA_DV, GLA_DK), F32)],
        compiler_params=_cparams("parallel", "arbitrary"),
        name="gla",
    )(gla_in, wup, bup, gout)


DSW_UNROLL = 4


def _dsw_kernel(q_ref, k_ref, v_ref, o_ref, acc_ref, lse_ref):
    s_len = q_ref.shape[1]
    qi = lax.broadcasted_iota(jnp.int32, (DSW_BLOCK, 2 * DSW_BLOCK), 0)
    ki = lax.broadcasted_iota(jnp.int32, (DSW_BLOCK, 2 * DSW_BLOCK), 1)
    rel = qi - ki
    head0 = lax.broadcasted_iota(jnp.int32, (DSW_BLOCK, LANES), 1) < DSW_HEAD_DIM

    for p, (window, d) in enumerate(DSW_PATTERNS):
        n_back = window // d
        nb = s_len // (d * DSW_BLOCK)

        def block(i, p=p, d=d, n_back=n_back, nb=nb):
            r = i // nb
            n = i % nb
            cur_off = jnp.where(n == 0, 0, DSW_BLOCK)
            q0 = r + d * DSW_BLOCK * n
            k0 = q0 - d * cur_off
            if d == 1:
                qrows = pl.ds(pl.multiple_of(q0, DSW_BLOCK), DSW_BLOCK)
                krows = pl.ds(pl.multiple_of(k0, DSW_BLOCK), 2 * DSW_BLOCK)
            else:
                qrows = pl.ds(q0, DSW_BLOCK, stride=d)
                krows = pl.ds(k0, 2 * DSW_BLOCK, stride=d)
            dist = rel + cur_off
            mask = (dist >= 0) & (dist <= n_back)
            q = q_ref[0, qrows, :]
            k = k_ref[0, krows, :].astype(BF16)
            v = v_ref[0, krows, :].astype(BF16)
            outs, lses = [], []
            for h in range(LANES // DSW_HEAD_DIM):
                qh = jnp.where(head0 if h == 0 else ~head0, q, 0.0).astype(BF16)
                s = lax.dot_general(qh, k, (((1,), (1,)), ((), ())), preferred_element_type=F32)
                s = jnp.where(mask, s * (DSW_HEAD_DIM ** -0.5), -jnp.inf)
                m = jnp.max(s, axis=-1, keepdims=True)
                e = jnp.exp(s - m)
                l = jnp.sum(e, axis=-1, keepdims=True)
                outs.append(jnp.dot(e.astype(BF16), v, preferred_element_type=F32) / l)
                lses.append(m + jnp.log(l))
            acc_ref[p, qrows, :] = jnp.where(head0, outs[0], outs[1])
            lse_ref[p, qrows, :] = jnp.where(head0, lses[0], lses[1])

        def blocks(g, carry, block=block):
            for u in range(DSW_UNROLL):
                block(g * DSW_UNROLL + u)
            return carry

        lax.fori_loop(0, s_len // (DSW_BLOCK * DSW_UNROLL), blocks, 0)

    l1, l2, l3 = lse_ref[0], lse_ref[1], lse_ref[2]
    m = jnp.maximum(jnp.maximum(l1, l2), l3)
    e1, e2, e3 = jnp.exp(l1 - m), jnp.exp(l2 - m), jnp.exp(l3 - m)
    o_ref[0] = (e1 * acc_ref[0] + e2 * acc_ref[1] + e3 * acc_ref[2]) / (e1 + e2 + e3)


def _dsw(att_in):
    b, s, _ = att_in.shape
    n_pat = len(DSW_PATTERNS)
    assert n_pat == 3 and all(s % (d * DSW_BLOCK) == 0 and s // (d * DSW_BLOCK) >= 2
                              for _, d in DSW_PATTERNS)
    n_hp = DSW_WIDTH // LANES
    blk = (1, s, LANES)
    in_spec = lambda off: pl.BlockSpec(blk, lambda i, hp: (i, 0, off + hp))
    return pl.pallas_call(
        _dsw_kernel,
        grid=(b, n_hp),
        in_specs=[in_spec(0), in_spec(n_hp), in_spec(2 * n_hp)],
        out_specs=pl.BlockSpec(blk, lambda i, hp: (i, 0, hp)),
        out_shape=jax.ShapeDtypeStruct((b, s, DSW_WIDTH), F32),
        scratch_shapes=[pltpu.VMEM((n_pat, s, LANES), F32), pltpu.VMEM((n_pat, s, LANES), F32)],
        compiler_params=_cparams("parallel", "parallel"),
        name="dsw",
    )(att_in, att_in, att_in)


MIX_TM = 256
PEER_GROUPS = PEER_HEADS * 2


def _mix_out_kernel(x_ref, oa_ref, ob_ref, wo_ref, gp_ref, wq_ref, sk_ref, x1_ref, h_ref, sc_ref):
    mix = (jnp.dot(oa_ref[...].astype(BF16), wo_ref[:GLA_WIDTH, :], preferred_element_type=F32)
           + jnp.dot(ob_ref[...].astype(BF16), wo_ref[GLA_WIDTH:, :], preferred_element_type=F32))
    x1 = x_ref[...] + mix
    x1_ref[...] = x1
    h = _rms(x1, gp_ref[...])
    for g in range(MIX_TM // SUBLANES):
        rows = slice(g * SUBLANES, (g + 1) * SUBLANES)
        slabs = _transpose8([h[rows, c * LANES:(c + 1) * LANES] for c in range(D_SUB)])
        for j in range(SUBLANES):
            h_ref[g * SUBLANES + j] = slabs[j]
    hb = h.astype(BF16)
    for g in range(PEER_GROUPS):
        qg = jnp.dot(hb, wq_ref[:, g * PEER_HALF:(g + 1) * PEER_HALF], preferred_element_type=F32)
        sc_ref[g] = lax.dot_general(sk_ref[g], qg.astype(BF16), (((1,), (1,)), ((), ())),
                                    preferred_element_type=F32)


def _mix_out(x2, oa, ob, wo, gp, wq, sk):
    t = x2.shape[0]
    tok = lambda w: pl.BlockSpec((MIX_TM, w), lambda i: (i, 0))
    full = lambda shape: pl.BlockSpec(shape, lambda i: (0,) * len(shape))
    return pl.pallas_call(
        _mix_out_kernel,
        grid=(t // MIX_TM,),
        in_specs=[tok(D_MODEL), tok(GLA_WIDTH), tok(DSW_WIDTH),
                  full((D_MODEL, D_MODEL)), full((1, D_MODEL)), full((D_MODEL, PEER_QCOLS)),
                  full((PEER_GROUPS, PEER_N_KEYS, PEER_HALF))],
        out_specs=[tok(D_MODEL), pl.BlockSpec((MIX_TM, D_SUB, LANES), lambda i: (i, 0, 0)),
                   pl.BlockSpec((PEER_GROUPS, PEER_N_KEYS, MIX_TM), lambda i: (0, 0, i))],
        out_shape=[jax.ShapeDtypeStruct((t, D_MODEL), F32),
                   jax.ShapeDtypeStruct((t, D_SUB, LANES), F32),
                   jax.ShapeDtypeStruct((PEER_GROUPS, PEER_N_KEYS, t), F32)],
        compiler_params=_cparams("parallel"),
        name="mix_out",
    )(x2, oa, ob, wo, gp, wq, sk)


TOPK_TT = 128


def _top_rows(s, order, payload=None):
    vals, idxs, pays = [], [], []
    for _ in range(PEER_TOPK):
        m = jnp.max(s, axis=0, keepdims=True)
        i = jnp.min(jnp.where(s == m, order, jnp.inf), axis=0, keepdims=True)
        hit = order == i
        if payload is not None:
            pays.append(jnp.max(jnp.where(hit, payload, -1.0), axis=0, keepdims=True))
        s = jnp.where(hit, -jnp.inf, s)
        vals.append(m)
        idxs.append(i)
    out = (jnp.concatenate(vals, axis=0), jnp.concatenate(idxs, axis=0))
    if payload is not None:
        out = out + (jnp.concatenate(pays, axis=0),)
    return out


def _pair_candidates(v0, v1, e0, e1):
    n = PEER_TOPK
    half = SUBLANES // 2
    sub = lax.broadcasted_iota(jnp.int32, (SUBLANES, v0.shape[1]), 0)
    subf = sub.astype(F32)
    lo = sub < half
    sums, flats, experts = [], [], []
    for j in range(5):
        sums.append(v0[0:SUBLANES] + v1[j:j + 1])
        experts.append(e0[0:SUBLANES] + e1[j:j + 1])
        flats.append(subf * n + j)
    sums.append(v0[SUBLANES:n] + v1[0:1])
    experts.append(e0[SUBLANES:n] + e1[0:1])
    flats.append((subf + SUBLANES) * n)
    sums.append(v0[0:1] + v1[SUBLANES:n])
    experts.append(e0[0:1] + e1[SUBLANES:n])
    flats.append(subf + SUBLANES)
    shifted = lambda a: jnp.where(lo, pltpu.roll(a[0:SUBLANES], 3, axis=0), pltpu.roll(a[0:SUBLANES], 7, axis=0))
    unused = (sub % half) == half - 1
    sums.append(jnp.where(unused, -jnp.inf, jnp.where(lo, v0[0:1], v0[1:2]) + shifted(v1)))
    experts.append(jnp.where(lo, e0[0:1], e0[1:2]) + shifted(e1))
    flats.append(jnp.where(unused, 1e9, jnp.where(lo, subf + 5, subf + (n + 1))))
    cat = lambda xs: jnp.concatenate(xs, axis=0)
    return cat(sums), cat(flats), cat(experts)


def _peer_topk_kernel(sc_ref, idx_ref, gate_ref):
    rows = lax.broadcasted_iota(jnp.int32, (PEER_N_KEYS, TOPK_TT), 0).astype(F32)

    def head(h, carry):
        v0, i0 = _top_rows(sc_ref[2 * h], rows)
        v1, i1 = _top_rows(sc_ref[2 * h + 1], rows)
        cand, flat, cexp = _pair_candidates(v0, v1, i0 * float(PEER_N_KEYS), i1)
        best, _, expert = _top_rows(cand, flat, cexp)
        e = jnp.exp(best - best[0:1, :])
        gates = e / jnp.sum(e, axis=0, keepdims=True)
        r0 = pl.multiple_of(h * PEER_TOPK, PEER_TOPK)
        idx_ref[pl.ds(r0, PEER_TOPK), :] = expert.astype(jnp.int32)
        gate_ref[pl.ds(r0, PEER_TOPK), :] = gates
        return carry

    lax.fori_loop(0, PEER_HEADS, head, 0)


def _peer_topk(scores_t):
    t = scores_t.shape[-1]
    out_spec = pl.BlockSpec((PEER_PICKS, TOPK_TT), lambda i: (0, i))
    return pl.pallas_call(
        _peer_topk_kernel,
        grid=(t // TOPK_TT,),
        in_specs=[pl.BlockSpec((PEER_GROUPS, PEER_N_KEYS, TOPK_TT), lambda i: (0, 0, i))],
        out_specs=[out_spec, out_spec],
        out_shape=[jax.ShapeDtypeStruct((PEER_PICKS, t), jnp.int32),
                   jax.ShapeDtypeStruct((PEER_PICKS, t), F32)],
        compiler_params=_cparams("parallel"),
        name="peer_topk",
    )(scores_t)


FFN_TB = 64
FFN_NBUF = 8
ROW_SUB = 2 * D_MODEL // LANES
D_SUB = D_MODEL // LANES


PACK_TE = 512


def _pack_tables_kernel(u_ref, v_ref, o_ref):
    for g in range(PACK_TE // SUBLANES):
        rows = slice(g * SUBLANES, (g + 1) * SUBLANES)
        for half, ref in enumerate((u_ref, v_ref)):
            slabs = _transpose8([ref[rows, c * LANES:(c + 1) * LANES] for c in range(D_SUB)])
            for j in range(SUBLANES):
                o_ref[g * SUBLANES + j, half * D_SUB:(half + 1) * D_SUB, :] = slabs[j]


def _pack_tables(u, v):
    n_exp = u.shape[0]
    tab = pl.BlockSpec((PACK_TE, D_MODEL), lambda i: (i, 0))
    return pl.pallas_call(
        _pack_tables_kernel,
        grid=(n_exp // PACK_TE,),
        in_specs=[tab, tab],
        out_specs=pl.BlockSpec((PACK_TE, ROW_SUB, LANES), lambda i: (i, 0, 0)),
        out_shape=jax.ShapeDtypeStruct((n_exp, ROW_SUB, LANES), F32),
        compiler_params=_cparams("parallel"),
        name="pack_tables",
    )(u, v)


def _sublane_fold(vs):
    sub = lax.broadcasted_iota(jnp.int32, (SUBLANES, LANES), 0)
    span = SUBLANES // 2
    while len(vs) > 1:
        low = (sub % (2 * span)) < span
        nxt = []
        for j in range(len(vs) // 2):
            a, b = vs[j], vs[j + len(vs) // 2]
            if span == SUBLANES // 2:
                nxt.append(jnp.where(low, a, b) + pltpu.roll(jnp.where(low, b, a), span, axis=0))
            else:
                nxt.append(jnp.where(low, a + pltpu.roll(a, SUBLANES - span, axis=0),
                                     b + pltpu.roll(b, span, axis=0)))
        vs = nxt
        span //= 2
    return vs[0]


def _peer_ffn_kernel(idx_ref, idx_next_ref, h_ref, gt_ref, uv_ref, o_ref, buf, act_ref, sems):
    step = pl.program_id(0)
    last = pl.num_programs(0) - 1
    ahead = FFN_NBUF - 1

    def row_copy(src_idx_ref, row, k, slot):
        return pltpu.make_async_copy(uv_ref.at[src_idx_ref[row, k]], buf.at[slot, k], sems.at[slot])

    def wait(slot):
        pltpu.make_async_copy(uv_ref.at[pl.ds(0, PEER_PICKS)], buf.at[slot], sems.at[slot]).wait()

    @pl.when(step == 0)
    def _():
        def prime(t, carry):
            for k in range(PEER_PICKS):
                row_copy(idx_ref, t, k, t).start(priority=k % 2)
            return carry

        lax.fori_loop(0, ahead, prime, 0)

    lane = lax.broadcasted_iota(jnp.int32, (PEER_PICKS, LANES), 1)
    lane0 = (step * FFN_TB) % LANES

    def token(t, src_idx_ref, src_row):
        slot = t % FFN_NBUF
        nslot = (t + ahead) % FFN_NBUF

        def issue(k):
            row_copy(src_idx_ref, src_row, k, nslot).start(priority=k % 2)

        wait(slot)
        x = h_ref[t]
        packed = []
        for g in range(PEER_PICKS // SUBLANES):
            prods = []
            for j in range(SUBLANES):
                k = g * SUBLANES + j
                if k % 2 == 0:
                    issue(k // 2)
                prods.append(buf[slot, k, 0:D_SUB, :] * x)
            packed.append(_sublane_fold(prods))
        part = jnp.concatenate(packed, axis=0)
        a = jnp.sum(part, axis=-1, keepdims=True)
        gate = jnp.sum(jnp.where(lane == lane0 + t, gt_ref[...], 0.0), axis=-1, keepdims=True)
        act = 0.5 * a * (1.0 + lax.erf(a * (2.0 ** -0.5))) * gate
        act_ref[...] = jnp.broadcast_to(act, (PEER_PICKS, LANES))
        acc = jnp.zeros((D_SUB, LANES), F32)
        for k in range(PEER_PICKS):
            if k % 2 == 0:
                issue(PEER_PICKS // 2 + k // 2)
            acc = acc + act_ref[k:k + 1, :] * buf[slot, k, D_SUB:ROW_SUB, :]
        o_ref[t] = acc

    def same_step(t, carry):
        token(t, idx_ref, t + ahead)
        return carry

    def next_step(t, carry):
        token(t, idx_next_ref, t + ahead - FFN_TB)
        return carry

    lax.fori_loop(0, FFN_TB - ahead, same_step, 0)
    lax.fori_loop(FFN_TB - ahead, FFN_TB, next_step, 0)

    @pl.when(step == last)
    def _():
        for j in range(ahead):
            wait(j)


def _peer_ffn(idx, h3, gates_t, uv):
    t = idx.shape[0]
    n_steps = t // FFN_TB
    assert FFN_TB % FFN_NBUF == 0 and LANES % FFN_TB == 0
    idx_spec = lambda index_map: pl.BlockSpec((FFN_TB, PEER_PICKS), index_map, memory_space=pltpu.SMEM)
    return pl.pallas_call(
        _peer_ffn_kernel,
        grid=(n_steps,),
        in_specs=[idx_spec(lambda i: (i, 0)),
                  idx_spec(lambda i: (jnp.minimum(i + 1, n_steps - 1), 0)),
                  pl.BlockSpec((FFN_TB, D_SUB, LANES), lambda i: (i, 0, 0)),
                  pl.BlockSpec((PEER_PICKS, LANES), lambda i: (0, i * FFN_TB // LANES)),
                  pl.BlockSpec(memory_space=pl.ANY)],
        out_specs=pl.BlockSpec((FFN_TB, D_SUB, LANES), lambda i: (i, 0, 0)),
        out_shape=jax.ShapeDtypeStruct((t, D_SUB, LANES), F32),
        scratch_shapes=[pltpu.VMEM((FFN_NBUF, PEER_PICKS, ROW_SUB, LANES), F32),
                        pltpu.VMEM((PEER_PICKS, LANES), F32),
                        pltpu.SemaphoreType.DMA((FFN_NBUF,))],
        compiler_params=_cparams("arbitrary"),
        name="peer_ffn",
    )(idx, idx, h3, gates_t, uv)


FIN_TM = 256


def _final_kernel(x1_ref, peer_ref, p_ref, gple_ref, wg_ref, wp_ref, gfin_ref, o_ref):
    groups = []
    for g in range(FIN_TM // SUBLANES):
        cols = _transpose8([peer_ref[g * SUBLANES + j] for j in range(SUBLANES)])
        groups.append(jnp.concatenate(cols, axis=1))
    x2 = x1_ref[...] + jnp.concatenate(groups, axis=0)
    z = jnp.dot(_rms(x2, gple_ref[...]).astype(BF16), wg_ref[...], preferred_element_type=F32)
    gate = 1.0 / (1.0 + jnp.exp(-z))
    emb = jnp.dot(p_ref[...].astype(BF16), wp_ref[...], preferred_element_type=F32)
    o_ref[...] = _rms(x2 + emb * gate, gfin_ref[...])


def _final(x1, peer_out, p2, gple, wg, wp, gfin):
    t = x1.shape[0]
    tok = lambda w: pl.BlockSpec((FIN_TM, w), lambda i: (i, 0))
    full = lambda shape: pl.BlockSpec(shape, lambda i: (0, 0))
    return pl.pallas_call(
        _final_kernel,
        grid=(t // FIN_TM,),
        in_specs=[tok(D_MODEL), pl.BlockSpec((FIN_TM, D_SUB, LANES), lambda i: (i, 0, 0)),
                  tok(PLE_DIM), full((1, D_MODEL)),
                  full((D_MODEL, D_MODEL)), full((PLE_DIM, D_MODEL)), full((1, D_MODEL))],
        out_specs=tok(D_MODEL),
        out_shape=jax.ShapeDtypeStruct((t, D_MODEL), F32),
        compiler_params=_cparams("parallel"),
        name="final",
    )(x1, peer_out, p2, gple, wg, wp, gfin)


def _rope_lane_constants():
    lane = np.arange(LANES) % DSW_HEAD_DIM
    half = ROT_DIM // 2
    inv_freq = ROPE_THETA ** (-jnp.arange(0, ROT_DIM, 2, dtype=F32) / ROT_DIM)
    invf = jnp.where(lane < ROT_DIM, inv_freq[lane % half], 0.0).astype(F32)[None, :]
    sga = jnp.asarray(np.where(lane < half, -1.0, 0.0), F32)[None, :]
    sgb = jnp.asarray(np.where((lane >= half) & (lane < ROT_DIM), 1.0, 0.0), F32)[None, :]
    return invf, sga, sgb


def kernel(x, p, positions, g_mix, w_in, w_gla_gate_up, b_gla_gate, g_gla_out, w_out, g_peer,
           w_peer_q, peer_sub_keys, peer_u, peer_v, g_ple, w_ple_gate, w_ple_proj, g_final):
    bsz, slen, dm = x.shape
    t = bsz * slen
    x2 = x.reshape(t, dm)
    pos2 = positions.reshape(t, 1)

    w = w_in[0]
    c_gate = 2 * GLA_QK + 2 * GLA_WIDTH
    w_gla = jnp.concatenate(
        [w[:, :c_gate], jnp.pad(w[:, c_gate:c_gate + GLA_GATE_RANK], ((0, 0), (0, GATE_PAD - GLA_GATE_RANK)))],
        axis=1).astype(BF16)
    w_att = w[:, c_gate + GLA_GATE_RANK:].astype(BF16)
    invf, sga, sgb = _rope_lane_constants()
    gla_in, att_in = _in_proj(x2, pos2, g_mix[0][None, :], w_gla, w_att, invf, sga, sgb)

    wup = jnp.pad(w_gla_gate_up[0], ((0, GATE_PAD - GLA_GATE_RANK), (0, 0)))
    oa = _gla(gla_in.reshape(bsz, slen, GLA_IN_COLS), wup, b_gla_gate[0][None, :], g_gla_out[0])

    ob = _dsw(att_in.reshape(bsz, slen, ATT_IN_COLS))

    sub_keys = peer_sub_keys[0].reshape(PEER_GROUPS, PEER_N_KEYS, PEER_HALF).astype(BF16)
    x1, h, scores_t = _mix_out(
        x2, oa.reshape(t, GLA_WIDTH), ob.reshape(t, DSW_WIDTH),
        w_out[0].astype(BF16), g_peer[0][None, :], w_peer_q[0].astype(BF16), sub_keys)

    idx_t, gates_t = _peer_topk(scores_t)
    peer_out = _peer_ffn(idx_t.T, h, gates_t, _pack_tables(peer_u[0], peer_v[0]))

    out = _final(x1, peer_out, p[0].reshape(t, PLE_DIM), g_ple[0][None, :],
                 w_ple_gate[0].astype(BF16), w_ple_proj[0].astype(BF16), g_final[None, :])
    return out.reshape(bsz, slen, dm)
```

```python
import jax
import jax.numpy as jnp
import numpy as np
from jax import lax
from jax.experimental import pallas as pl
from jax.experimental.pallas import tpu as pltpu

F32 = jnp.float32
BF16 = jnp.bfloat16

D_MODEL = 1024
PLE_DIM = 256
NORM_EPS = 1e-6
GLA_HEADS = 4
GLA_DK = 64
GLA_DV = 128
GLA_QK = GLA_HEADS * GLA_DK
GLA_WIDTH = GLA_HEADS * GLA_DV
GLA_GATE_RANK = 16
GLA_TAU = 16.0
GLA_CHUNK = 64
DSW_HEAD_DIM = 64
DSW_HEADS = 8
DSW_WIDTH = DSW_HEADS * DSW_HEAD_DIM
DSW_PATTERNS = ((128, 1), (512, 4), (2048, 16))
DSW_BLOCK = 128
ROT_DIM = 16
ROPE_THETA = 500000.0
PEER_N_KEYS = 128
PEER_HEADS = 8
PEER_TOPK = 16
PEER_HALF = 128
PEER_QCOLS = PEER_HEADS * 2 * PEER_HALF
PEER_PICKS = PEER_HEADS * PEER_TOPK

LANES = 128
SUBLANES = 8
GATE_PAD = LANES
GLA_IN_COLS = 2 * GLA_QK + 2 * GLA_WIDTH + GATE_PAD
ATT_IN_COLS = 3 * DSW_WIDTH
D_SUB = D_MODEL // LANES
ROW_SUB = 2 * D_SUB

VMEM_LIMIT = 56 * 1024 * 1024


def _cparams(*sem):
    return pltpu.CompilerParams(dimension_semantics=sem, vmem_limit_bytes=VMEM_LIMIT)


def _rms(x, g):
    return x * lax.rsqrt(jnp.mean(x * x, axis=-1, keepdims=True) + NORM_EPS) * g


def _transpose8(vs):
    sub = lax.broadcasted_iota(jnp.int32, (SUBLANES, LANES), 0)
    vs = list(vs)
    span = SUBLANES // 2
    while span >= 1:
        low = (sub % (2 * span)) < span
        for c in range(SUBLANES):
            if c % (2 * span) < span:
                a, b = vs[c], vs[c + span]
                vs[c] = jnp.where(low, a, pltpu.roll(b, span, axis=0))
                vs[c + span] = jnp.where(low, pltpu.roll(a, SUBLANES - span, axis=0), b)
        span //= 2
    return vs


IN_TM = 512


def _in_proj_kernel(x_ref, pos_ref, g_ref, wg_ref, wa_ref, invf_ref, sga_ref, sgb_ref,
                    gla_ref, att_ref):
    h = _rms(x_ref[...], g_ref[...]).astype(BF16)
    gla_ref[...] = jnp.dot(h, wg_ref[...], preferred_element_type=F32)
    ang = pos_ref[...].astype(F32) * invf_ref[...]
    cos = jnp.cos(ang)
    sin = jnp.sin(ang)
    sin_a = sin * sga_ref[...]
    sin_b = sin * sgb_ref[...]
    half = ROT_DIM // 2
    for c in range(ATT_IN_COLS // LANES):
        y = jnp.dot(h, wa_ref[:, c * LANES:(c + 1) * LANES], preferred_element_type=F32)
        if c < 2 * DSW_WIDTH // LANES:
            y = (y * cos + pltpu.roll(y, LANES - half, axis=1) * sin_a
                 + pltpu.roll(y, half, axis=1) * sin_b)
        att_ref[:, c * LANES:(c + 1) * LANES] = y


def _in_proj(x2, pos2, g, wg, wa, invf, sga, sgb):
    t = x2.shape[0]
    full = lambda shape: pl.BlockSpec(shape, lambda i: (0, 0))
    return pl.pallas_call(
        _in_proj_kernel,
        grid=(t // IN_TM,),
        in_specs=[pl.BlockSpec((IN_TM, D_MODEL), lambda i: (i, 0)),
                  pl.BlockSpec((IN_TM, 1), lambda i: (i, 0)),
                  full((1, D_MODEL)), full((D_MODEL, GLA_IN_COLS)), full((D_MODEL, ATT_IN_COLS)),
                  full((1, LANES)), full((1, LANES)), full((1, LANES))],
        out_specs=[pl.BlockSpec((IN_TM, GLA_IN_COLS), lambda i: (i, 0)),
                   pl.BlockSpec((IN_TM, ATT_IN_COLS), lambda i: (i, 0))],
        out_shape=[jax.ShapeDtypeStruct((t, GLA_IN_COLS), F32),
                   jax.ShapeDtypeStruct((t, ATT_IN_COLS), F32)],
        compiler_params=_cparams("parallel"),
        name="in_proj",
    )(x2, pos2, g, wg, wa, invf, sga, sgb)


GLA_SB = 512
GLA_UNROLL = 2


def _log_sigmoid(z):
    return jnp.minimum(z, 0.0) - jnp.log(1.0 + jnp.exp(-jnp.abs(z)))


def _gla_kernel(in_ref, wup_ref, bup_ref, gout_ref, o_ref, state_ref):
    @pl.when(pl.program_id(1) == 0)
    def _():
        state_ref[...] = jnp.zeros_like(state_ref)

    c_i = lax.broadcasted_iota(jnp.int32, (GLA_CHUNK, GLA_CHUNK), 0)
    c_j = lax.broadcasted_iota(jnp.int32, (GLA_CHUNK, GLA_CHUNK), 1)
    causal = c_j <= c_i
    tri = causal.astype(F32)
    o_q, o_k, o_v, o_r, o_g = 0, GLA_QK, 2 * GLA_QK, 2 * GLA_QK + GLA_WIDTH, 2 * GLA_QK + 2 * GLA_WIDTH

    def chunk(c):
        r0 = pl.multiple_of(c * GLA_CHUNK, GLA_CHUNK)
        rows = pl.ds(r0, GLA_CHUNK)
        code = in_ref[0, rows, o_g:o_g + GATE_PAD]
        pre = jnp.dot(code, wup_ref[...], preferred_element_type=F32) + bup_ref[...]
        log_a = _log_sigmoid(pre) * (1.0 / GLA_TAU)
        b = jnp.dot(tri, log_a, preferred_element_type=F32,
                    precision=lax.Precision.HIGHEST)
        b_last = b[GLA_CHUNK - 1:GLA_CHUNK, :]
        q = in_ref[0, rows, o_q:o_q + GLA_QK]
        k = in_ref[0, rows, o_k:o_k + GLA_QK]
        q_dec = q * (GLA_DK ** -0.5) * jnp.exp(b)
        k_dec = k * jnp.exp(-b)
        k_end = k * jnp.exp(b_last - b)
        decay = jnp.exp(b_last)
        for h in range(GLA_HEADS):
            ks = slice(h * GLA_DK, (h + 1) * GLA_DK)
            vs = slice(h * GLA_DV, (h + 1) * GLA_DV)
            v = in_ref[0, rows, o_v + h * GLA_DV:o_v + (h + 1) * GLA_DV]
            r = in_ref[0, rows, o_r + h * GLA_DV:o_r + (h + 1) * GLA_DV]
            qd = q_dec[:, ks]
            attn = lax.dot_general(qd, k_dec[:, ks], (((1,), (1,)), ((), ())),
                                   preferred_element_type=F32)
            attn = jnp.where(causal, attn, 0.0)
            st = state_ref[h]
            o = (jnp.dot(attn, v, preferred_element_type=F32)
                 + lax.dot_general(qd, st, (((1,), (1,)), ((), ())), preferred_element_type=F32))
            kv_t = lax.dot_general(v, k_end[:, ks], (((0,), (0,)), ((), ())),
                                   preferred_element_type=F32)
            state_ref[h] = st * decay[:, ks] + kv_t
            o = _rms(o, gout_ref[h:h + 1, :])
            o_ref[0, rows, vs] = o * (r / (1.0 + jnp.exp(-r)))

    def chunks(g, carry):
        for u in range(GLA_UNROLL):
            chunk(g * GLA_UNROLL + u)
        return carry

    lax.fori_loop(0, GLA_SB // (GLA_CHUNK * GLA_UNROLL), chunks, 0)


def _gla(gla_in, wup, bup, gout):
    b, s, _ = gla_in.shape
    full = lambda shape: pl.BlockSpec(shape, lambda i, j: (0,) * len(shape))
    return pl.pallas_call(
        _gla_kernel,
        grid=(b, s // GLA_SB),
        in_specs=[pl.BlockSpec((1, GLA_SB, GLA_IN_COLS), lambda i, j: (i, j, 0)),
                  full((GATE_PAD, GLA_QK)), full((1, GLA_QK)), full((GLA_HEADS, GLA_DV))],
        out_specs=pl.BlockSpec((1, GLA_SB, GLA_WIDTH), lambda i, j: (i, j, 0)),
        out_shape=jax.ShapeDtypeStruct((b, s, GLA_WIDTH), F32),
        scratch_shapes=[pltpu.VMEM((GLA_HEADS, GLA_DV, GLA_DK), F32)],
        compiler_params=_cparams("parallel", "arbitrary"),
        name="gla",
    )(gla_in, wup, bup, gout)


DSW_UNROLL = 4


def _dsw_kernel(q_ref, k_ref, v_ref, o_ref, acc_ref, lse_ref):
    s_len = q_ref.shape[1]
    qi = lax.broadcasted_iota(jnp.int32, (DSW_BLOCK, 2 * DSW_BLOCK), 0)
    ki = lax.broadcasted_iota(jnp.int32, (DSW_BLOCK, 2 * DSW_BLOCK), 1)
    rel = qi - ki
    head0 = lax.broadcasted_iota(jnp.int32, (DSW_BLOCK, LANES), 1) < DSW_HEAD_DIM

    for p, (window, d) in enumerate(DSW_PATTERNS):
        n_back = window // d
        nb = s_len // (d * DSW_BLOCK)

        def block(i, p=p, d=d, n_back=n_back, nb=nb):
            r = i // nb
            n = i % nb
            cur_off = jnp.where(n == 0, 0, DSW_BLOCK)
            q0 = r + d * DSW_BLOCK * n
            k0 = q0 - d * cur_off
            if d == 1:
                qrows = pl.ds(pl.multiple_of(q0, DSW_BLOCK), DSW_BLOCK)
                krows = pl.ds(pl.multiple_of(k0, DSW_BLOCK), 2 * DSW_BLOCK)
            else:
                qrows = pl.ds(q0, DSW_BLOCK, stride=d)
                krows = pl.ds(k0, 2 * DSW_BLOCK, stride=d)
            dist = rel + cur_off
            mask = (dist >= 0) & (dist <= n_back)
            q = q_ref[0, qrows, :]
            k = k_ref[0, krows, :].astype(BF16)
            v = v_ref[0, krows, :].astype(BF16)
            outs, lses = [], []
            for h in range(LANES // DSW_HEAD_DIM):
                qh = jnp.where(head0 if h == 0 else ~head0, q, 0.0).astype(BF16)
                s = lax.dot_general(qh, k, (((1,), (1,)), ((), ())), preferred_element_type=F32)
                s = jnp.where(mask, s * (DSW_HEAD_DIM ** -0.5), -jnp.inf)
                m = jnp.max(s, axis=-1, keepdims=True)
                e = jnp.exp(s - m)
                l = jnp.sum(e, axis=-1, keepdims=True)
                outs.append(jnp.dot(e.astype(BF16), v, preferred_element_type=F32) / l)
                lses.append(m + jnp.log(l))
            acc_ref[p, qrows, :] = jnp.where(head0, outs[0], outs[1])
            lse_ref[p, qrows, :] = jnp.where(head0, lses[0], lses[1])

        def blocks(g, carry, block=block):
            for u in range(DSW_UNROLL):
                block(g * DSW_UNROLL + u)
            return carry

        lax.fori_loop(0, s_len // (DSW_BLOCK * DSW_UNROLL), blocks, 0)

    l1, l2, l3 = lse_ref[0], lse_ref[1], lse_ref[2]
    m = jnp.maximum(jnp.maximum(l1, l2), l3)
    e1, e2, e3 = jnp.exp(l1 - m), jnp.exp(l2 - m), jnp.exp(l3 - m)
    o_ref[0] = (e1 * acc_ref[0] + e2 * acc_ref[1] + e3 * acc_ref[2]) / (e1 + e2 + e3)


def _dsw(att_in):
    b, s, _ = att_in.shape
    n_pat = len(DSW_PATTERNS)
    assert n_pat == 3 and all(s % (d * DSW_BLOCK) == 0 and s // (d * DSW_BLOCK) >= 2
                              for _, d in DSW_PATTERNS)
    n_hp = DSW_WIDTH // LANES
    blk = (1, s, LANES)
    in_spec = lambda off: pl.BlockSpec(blk, lambda i, hp: (i, 0, off + hp))
    return pl.pallas_call(
        _dsw_kernel,
        grid=(b, n_hp),
        in_specs=[in_spec(0), in_spec(n_hp), in_spec(2 * n_hp)],
        out_specs=pl.BlockSpec(blk, lambda i, hp: (i, 0, hp)),
        out_shape=jax.ShapeDtypeStruct((b, s, DSW_WIDTH), F32),
        scratch_shapes=[pltpu.VMEM((n_pat, s, LANES), F32), pltpu.VMEM((n_pat, s, LANES), F32)],
        compiler_params=_cparams("parallel", "parallel"),
        name="dsw",
    )(att_in, att_in, att_in)


MIX_TM = 256
PEER_GROUPS = PEER_HEADS * 2


def _mix_out_kernel(x_ref, oa_ref, ob_ref, wo_ref, gp_ref, wq_ref, sk_ref, x1_ref, h_ref, sc_ref):
    mix = (jnp.dot(oa_ref[...].astype(BF16), wo_ref[:GLA_WIDTH, :], preferred_element_type=F32)
           + jnp.dot(ob_ref[...].astype(BF16), wo_ref[GLA_WIDTH:, :], preferred_element_type=F32))
    x1 = x_ref[...] + mix
    x1_ref[...] = x1
    h = _rms(x1, gp_ref[...])
    for g in range(MIX_TM // SUBLANES):
        rows = slice(g * SUBLANES, (g + 1) * SUBLANES)
        slabs = _transpose8([h[rows, c * LANES:(c + 1) * LANES] for c in range(D_SUB)])
        for j in range(SUBLANES):
            h_ref[g * SUBLANES + j] = slabs[j]
    hb = h.astype(BF16)
    for g in range(PEER_GROUPS):
        qg = jnp.dot(hb, wq_ref[:, g * PEER_HALF:(g + 1) * PEER_HALF], preferred_element_type=F32)
        sc_ref[g] = lax.dot_general(sk_ref[g], qg.astype(BF16), (((1,), (1,)), ((), ())),
                                    preferred_element_type=F32)


def _mix_out(x2, oa, ob, wo, gp, wq, sk):
    t = x2.shape[0]
    tok = lambda w: pl.BlockSpec((MIX_TM, w), lambda i: (i, 0))
    full = lambda shape: pl.BlockSpec(shape, lambda i: (0,) * len(shape))
    return pl.pallas_call(
        _mix_out_kernel,
        grid=(t // MIX_TM,),
        in_specs=[tok(D_MODEL), tok(GLA_WIDTH), tok(DSW_WIDTH),
                  full((D_MODEL, D_MODEL)), full((1, D_MODEL)), full((D_MODEL, PEER_QCOLS)),
                  full((PEER_GROUPS, PEER_N_KEYS, PEER_HALF))],
        out_specs=[tok(D_MODEL), pl.BlockSpec((MIX_TM, D_SUB, LANES), lambda i: (i, 0, 0)),
                   pl.BlockSpec((PEER_GROUPS, PEER_N_KEYS, MIX_TM), lambda i: (0, 0, i))],
        out_shape=[jax.ShapeDtypeStruct((t, D_MODEL), F32),
                   jax.ShapeDtypeStruct((t, D_SUB, LANES), F32),
                   jax.ShapeDtypeStruct((PEER_GROUPS, PEER_N_KEYS, t), F32)],
        compiler_params=_cparams("parallel"),
        name="mix_out",
    )(x2, oa, ob, wo, gp, wq, sk)


TOPK_TT = 128


def _top_rows(s, order, payload=None):
    vals, idxs, pays = [], [], []
    for _ in range(PEER_TOPK):
        m = jnp.max(s, axis=0, keepdims=True)
        i = jnp.min(jnp.where(s == m, order, jnp.inf), axis=0, keepdims=True)
        hit = order == i
        if payload is not None:
            pays.append(jnp.max(jnp.where(hit, payload, -1.0), axis=0, keepdims=True))
        s = jnp.where(hit, -jnp.inf, s)
        vals.append(m)
        idxs.append(i)
    out = (jnp.concatenate(vals, axis=0), jnp.concatenate(idxs, axis=0))
    if payload is not None:
        out = out + (jnp.concatenate(pays, axis=0),)
    return out


def _pair_candidates(v0, v1, e0, e1):
    n = PEER_TOPK
    half = SUBLANES // 2
    sub = lax.broadcasted_iota(jnp.int32, (SUBLANES, v0.shape[1]), 0)
    subf = sub.astype(F32)
    lo = sub < half
    sums, flats, experts = [], [], []
    for j in range(5):
        sums.append(v0[0:SUBLANES] + v1[j:j + 1])
        experts.append(e0[0:SUBLANES] + e1[j:j + 1])
        flats.append(subf * n + j)
    sums.append(v0[SUBLANES:n] + v1[0:1])
    experts.append(e0[SUBLANES:n] + e1[0:1])
    flats.append((subf + SUBLANES) * n)
    sums.append(v0[0:1] + v1[SUBLANES:n])
    experts.append(e0[0:1] + e1[SUBLANES:n])
    flats.append(subf + SUBLANES)
    shifted = lambda a: jnp.where(lo, pltpu.roll(a[0:SUBLANES], 3, axis=0), pltpu.roll(a[0:SUBLANES], 7, axis=0))
    unused = (sub % half) == half - 1
    sums.append(jnp.where(unused, -jnp.inf, jnp.where(lo, v0[0:1], v0[1:2]) + shifted(v1)))
    experts.append(jnp.where(lo, e0[0:1], e0[1:2]) + shifted(e1))
    flats.append(jnp.where(unused, 1e9, jnp.where(lo, subf + 5, subf + (n + 1))))
    cat = lambda xs: jnp.concatenate(xs, axis=0)
    return cat(sums), cat(flats), cat(experts)


def _peer_topk_kernel(sc_ref, idx_ref, gate_ref):
    rows = lax.broadcasted_iota(jnp.int32, (PEER_N_KEYS, TOPK_TT), 0).astype(F32)

    def head(h, carry):
        v0, i0 = _top_rows(sc_ref[2 * h], rows)
        v1, i1 = _top_rows(sc_ref[2 * h + 1], rows)
        cand, flat, cexp = _pair_candidates(v0, v1, i0 * float(PEER_N_KEYS), i1)
        best, _, expert = _top_rows(cand, flat, cexp)
        e = jnp.exp(best - best[0:1, :])
        gates = e / jnp.sum(e, axis=0, keepdims=True)
        r0 = pl.multiple_of(h * PEER_TOPK, PEER_TOPK)
        idx_ref[pl.ds(r0, PEER_TOPK), :] = expert.astype(jnp.int32)
        gate_ref[pl.ds(r0, PEER_TOPK), :] = gates
        return carry

    lax.fori_loop(0, PEER_HEADS, head, 0)


def _peer_topk(scores_t):
    t = scores_t.shape[-1]
    out_spec = pl.BlockSpec((PEER_PICKS, TOPK_TT), lambda i: (0, i))
    return pl.pallas_call(
        _peer_topk_kernel,
        grid=(t // TOPK_TT,),
        in_specs=[pl.BlockSpec((PEER_GROUPS, PEER_N_KEYS, TOPK_TT), lambda i: (0, 0, i))],
        out_specs=[out_spec, out_spec],
        out_shape=[jax.ShapeDtypeStruct((PEER_PICKS, t), jnp.int32),
                   jax.ShapeDtypeStruct((PEER_PICKS, t), F32)],
        compiler_params=_cparams("parallel"),
        name="peer_topk",
    )(scores_t)


PACK_TE = 512


def _pack_tables_kernel(u_ref, v_ref, o_ref):
    for g in range(PACK_TE // SUBLANES):
        rows = slice(g * SUBLANES, (g + 1) * SUBLANES)
        for half, ref in enumerate((u_ref, v_ref)):
            slabs = _transpose8([ref[rows, c * LANES:(c + 1) * LANES] for c in range(D_SUB)])
            for j in range(SUBLANES):
                o_ref[g * SUBLANES + j, half * D_SUB:(half + 1) * D_SUB, :] = slabs[j]


def _pack_tables(u, v):
    n_exp = u.shape[0]
    tab = pl.BlockSpec((PACK_TE, D_MODEL), lambda i: (i, 0))
    return pl.pallas_call(
        _pack_tables_kernel,
        grid=(n_exp // PACK_TE,),
        in_specs=[tab, tab],
        out_specs=pl.BlockSpec((PACK_TE, ROW_SUB, LANES), lambda i: (i, 0, 0)),
        out_shape=jax.ShapeDtypeStruct((n_exp, ROW_SUB, LANES), F32),
        compiler_params=_cparams("parallel"),
        name="pack_tables",
    )(u, v)


FFN_TB = 64
FFN_NBUF = 16
FFN_AHEAD = 14


def _sublane_fold(vs):
    sub = lax.broadcasted_iota(jnp.int32, (SUBLANES, LANES), 0)
    span = SUBLANES // 2
    while len(vs) > 1:
        low = (sub % (2 * span)) < span
        nxt = []
        for j in range(len(vs) // 2):
            a, b = vs[j], vs[j + len(vs) // 2]
            if span == SUBLANES // 2:
                nxt.append(jnp.where(low, a, b) + pltpu.roll(jnp.where(low, b, a), span, axis=0))
            else:
                nxt.append(jnp.where(low, a + pltpu.roll(a, SUBLANES - span, axis=0),
                                     b + pltpu.roll(b, span, axis=0)))
        vs = nxt
        span //= 2
    return vs[0]


def _peer_ffn_kernel(idx_ref, idx_next_ref, h_ref, gt_ref, uv_ref, o_ref, buf_e, buf_o, act_ref, sems):
    step = pl.program_id(0)
    last = pl.num_programs(0) - 1
    ahead = FFN_AHEAD
    bufs = (buf_e, buf_o)
    pairs_per_buf = FFN_NBUF // 4

    def ring_pos(t):
        return ((t // 4) % pairs_per_buf) * 2 + t % 2

    def row_copy(src_idx_ref, row, k, t_dst, parity):
        return pltpu.make_async_copy(uv_ref.at[src_idx_ref[row, k]],
                                     bufs[parity].at[ring_pos(t_dst), k], sems.at[t_dst % FFN_NBUF])

    def wait(t, parity):
        pltpu.make_async_copy(uv_ref.at[pl.ds(0, PEER_PICKS)], bufs[parity].at[ring_pos(t)],
                              sems.at[t % FFN_NBUF]).wait()

    @pl.when(step == 0)
    def _():
        for t in range(ahead):
            def prime(k2, carry, t=t):
                row_copy(idx_ref, t, 2 * k2, t, (t // 2) % 2).start(priority=0)
                row_copy(idx_ref, t, 2 * k2 + 1, t, (t // 2) % 2).start(priority=1)
                return carry

            lax.fori_loop(0, PEER_PICKS // 2, prime, 0)

    lane = lax.broadcasted_iota(jnp.int32, (PEER_PICKS, LANES), 1)
    lane0 = (step * FFN_TB) % LANES

    def token_pair(t0, parity, src_idx_ref, src_row0):
        buf = bufs[parity]
        slots = (ring_pos(t0), ring_pos(t0 + 1))
        n_half = PEER_PICKS // 2

        def issue(which, k):
            row_copy(src_idx_ref, src_row0 + which, k, t0 + which + ahead,
                     1 - parity).start(priority=k % 2)

        def dots(which):
            x = h_ref[t0 + which]
            packed = []
            for g in range(PEER_PICKS // SUBLANES):
                prods = []
                for j in range(SUBLANES):
                    k = g * SUBLANES + j
                    if k % 2 == 0:
                        issue(which, k // 2)
                    prods.append(buf[slots[which], k, 0:D_SUB, :] * x)
                packed.append(_sublane_fold(prods))
            part = jnp.concatenate(packed, axis=0)
            a = jnp.sum(part, axis=-1, keepdims=True)
            gate = jnp.sum(jnp.where(lane == lane0 + t0 + which, gt_ref[...], 0.0),
                           axis=-1, keepdims=True)
            act = 0.5 * a * (1.0 + lax.erf(a * (2.0 ** -0.5))) * gate
            act_ref[which] = jnp.broadcast_to(act, (PEER_PICKS, LANES))

        def weighted_sum(which):
            acc = jnp.zeros((D_SUB, LANES), F32)
            for k in range(PEER_PICKS):
                if k % 2 == 0:
                    issue(1 - which, n_half + k // 2)
                acc = acc + act_ref[which, k:k + 1, :] * buf[slots[which], k, D_SUB:ROW_SUB, :]
            o_ref[t0 + which] = acc

        wait(t0, parity)
        wait(t0 + 1, parity)
        dots(0)
        dots(1)
        weighted_sum(0)
        weighted_sum(1)

    def quad(i, src_a, src_b):
        t0 = 4 * i
        token_pair(t0, 0, src_a[0], t0 + ahead + src_a[1])
        token_pair(t0 + 2, 1, src_b[0], t0 + 2 + ahead + src_b[1])

    here, nxt = (idx_ref, 0), (idx_next_ref, -FFN_TB)
    n_quads = FFN_TB // 4
    split = (FFN_TB - ahead) // 4

    def same_step(i, carry):
        quad(i, here, here)
        return carry

    def next_step(i, carry):
        quad(i, nxt, nxt)
        return carry

    lax.fori_loop(0, split, same_step, 0)
    quad(split, here, nxt)
    lax.fori_loop(split + 1, n_quads, next_step, 0)

    @pl.when(step == last)
    def _():
        for t in range(ahead):
            wait(t, (t // 2) % 2)


def _peer_ffn(idx, h3, gates_t, uv):
    t = idx.shape[0]
    n_steps = t // FFN_TB
    assert FFN_TB % FFN_NBUF == 0 and FFN_NBUF % 4 == 0 and LANES % FFN_TB == 0
    assert FFN_AHEAD % 4 == 2 and FFN_AHEAD <= FFN_NBUF - 2
    idx_spec = lambda index_map: pl.BlockSpec((FFN_TB, PEER_PICKS), index_map, memory_space=pltpu.SMEM)
    return pl.pallas_call(
        _peer_ffn_kernel,
        grid=(n_steps,),
        in_specs=[idx_spec(lambda i: (i, 0)),
                  idx_spec(lambda i: (jnp.minimum(i + 1, n_steps - 1), 0)),
                  pl.BlockSpec((FFN_TB, D_SUB, LANES), lambda i: (i, 0, 0)),
                  pl.BlockSpec((PEER_PICKS, LANES), lambda i: (0, i * FFN_TB // LANES)),
                  pl.BlockSpec(memory_space=pl.ANY)],
        out_specs=pl.BlockSpec((FFN_TB, D_SUB, LANES), lambda i: (i, 0, 0)),
        out_shape=jax.ShapeDtypeStruct((t, D_SUB, LANES), F32),
        scratch_shapes=[pltpu.VMEM((FFN_NBUF // 2, PEER_PICKS, ROW_SUB, LANES), F32),
                        pltpu.VMEM((FFN_NBUF // 2, PEER_PICKS, ROW_SUB, LANES), F32),
                        pltpu.VMEM((2, PEER_PICKS, LANES), F32),
                        pltpu.SemaphoreType.DMA((FFN_NBUF,))],
        compiler_params=_cparams("arbitrary"),
        name="peer_ffn",
    )(idx, idx, h3, gates_t, uv)


FIN_TM = 256


def _final_kernel(x1_ref, peer_ref, p_ref, gple_ref, wg_ref, wp_ref, gfin_ref, o_ref):
    groups = []
    for g in range(FIN_TM // SUBLANES):
        cols = _transpose8([peer_ref[g * SUBLANES + j] for j in range(SUBLANES)])
        groups.append(jnp.concatenate(cols, axis=1))
    x2 = x1_ref[...] + jnp.concatenate(groups, axis=0)
    z = jnp.dot(_rms(x2, gple_ref[...]).astype(BF16), wg_ref[...], preferred_element_type=F32)
    gate = 1.0 / (1.0 + jnp.exp(-z))
    emb = jnp.dot(p_ref[...].astype(BF16), wp_ref[...], preferred_element_type=F32)
    o_ref[...] = _rms(x2 + emb * gate, gfin_ref[...])


def _final(x1, peer_out, p2, gple, wg, wp, gfin):
    t = x1.shape[0]
    tok = lambda w: pl.BlockSpec((FIN_TM, w), lambda i: (i, 0))
    full = lambda shape: pl.BlockSpec(shape, lambda i: (0, 0))
    return pl.pallas_call(
        _final_kernel,
        grid=(t // FIN_TM,),
        in_specs=[tok(D_MODEL), pl.BlockSpec((FIN_TM, D_SUB, LANES), lambda i: (i, 0, 0)),
                  tok(PLE_DIM), full((1, D_MODEL)),
                  full((D_MODEL, D_MODEL)), full((PLE_DIM, D_MODEL)), full((1, D_MODEL))],
        out_specs=tok(D_MODEL),
        out_shape=jax.ShapeDtypeStruct((t, D_MODEL), F32),
        compiler_params=_cparams("parallel"),
        name="final",
    )(x1, peer_out, p2, gple, wg, wp, gfin)


def _rope_lane_constants():
    lane = np.arange(LANES) % DSW_HEAD_DIM
    half = ROT_DIM // 2
    inv_freq = ROPE_THETA ** (-jnp.arange(0, ROT_DIM, 2, dtype=F32) / ROT_DIM)
    invf = jnp.where(lane < ROT_DIM, inv_freq[lane % half], 0.0).astype(F32)[None, :]
    sga = jnp.asarray(np.where(lane < half, -1.0, 0.0), F32)[None, :]
    sgb = jnp.asarray(np.where((lane >= half) & (lane < ROT_DIM), 1.0, 0.0), F32)[None, :]
    return invf, sga, sgb


def kernel(x, p, positions, g_mix, w_in, w_gla_gate_up, b_gla_gate, g_gla_out, w_out, g_peer,
           w_peer_q, peer_sub_keys, peer_u, peer_v, g_ple, w_ple_gate, w_ple_proj, g_final):
    bsz, slen, dm = x.shape
    t = bsz * slen
    x2 = x.reshape(t, dm)
    pos2 = positions.reshape(t, 1)

    w = w_in[0]
    c_gate = 2 * GLA_QK + 2 * GLA_WIDTH
    w_gla = jnp.concatenate(
        [w[:, :c_gate], jnp.pad(w[:, c_gate:c_gate + GLA_GATE_RANK], ((0, 0), (0, GATE_PAD - GLA_GATE_RANK)))],
        axis=1).astype(BF16)
    w_att = w[:, c_gate + GLA_GATE_RANK:].astype(BF16)
    invf, sga, sgb = _rope_lane_constants()
    gla_in, att_in = _in_proj(x2, pos2, g_mix[0][None, :], w_gla, w_att, invf, sga, sgb)

    wup = jnp.pad(w_gla_gate_up[0], ((0, GATE_PAD - GLA_GATE_RANK), (0, 0)))
    oa = _gla(gla_in.reshape(bsz, slen, GLA_IN_COLS), wup, b_gla_gate[0][None, :], g_gla_out[0])

    ob = _dsw(att_in.reshape(bsz, slen, ATT_IN_COLS))

    sub_keys = peer_sub_keys[0].reshape(PEER_GROUPS, PEER_N_KEYS, PEER_HALF).astype(BF16)
    x1, h, scores_t = _mix_out(
        x2, oa.reshape(t, GLA_WIDTH), ob.reshape(t, DSW_WIDTH),
        w_out[0].astype(BF16), g_peer[0][None, :], w_peer_q[0].astype(BF16), sub_keys)

    idx_t, gates_t = _peer_topk(scores_t)
    peer_out = _peer_ffn(idx_t.T, h, gates_t, _pack_tables(peer_u[0], peer_v[0]))

    out = _final(x1, peer_out, p[0].reshape(t, PLE_DIM), g_ple[0][None, :],
                 w_ple_gate[0].astype(BF16), w_ple_proj[0].astype(BF16), g_final[None, :])
    return out.reshape(bsz, slen, dm)
```

```python
import jax
import jax.numpy as jnp
import numpy as np
from jax import lax
from jax.experimental import pallas as pl
from jax.experimental.pallas import tpu as pltpu

F32 = jnp.float32
BF16 = jnp.bfloat16

D_MODEL = 1024
PLE_DIM = 256
NORM_EPS = 1e-6
GLA_HEADS = 4
GLA_DK = 64
GLA_DV = 128
GLA_QK = GLA_HEADS * GLA_DK
GLA_WIDTH = GLA_HEADS * GLA_DV
GLA_GATE_RANK = 16
GLA_TAU = 16.0
GLA_CHUNK = 64
DSW_HEAD_DIM = 64
DSW_HEADS = 8
DSW_WIDTH = DSW_HEADS * DSW_HEAD_DIM
DSW_PATTERNS = ((128, 1), (512, 4), (2048, 16))
DSW_BLOCK = 128
ROT_DIM = 16
ROPE_THETA = 500000.0
PEER_N_KEYS = 128
PEER_HEADS = 8
PEER_TOPK = 16
PEER_HALF = 128
PEER_QCOLS = PEER_HEADS * 2 * PEER_HALF
PEER_PICKS = PEER_HEADS * PEER_TOPK

LANES = 128
SUBLANES = 8
MXU_COLS = 256
GATE_PAD = LANES
GLA_IN_COLS = 2 * GLA_QK + 2 * GLA_WIDTH + GATE_PAD
ATT_IN_COLS = 3 * DSW_WIDTH
D_SUB = D_MODEL // LANES
ROW_SUB = 2 * D_SUB

VMEM_LIMIT = 56 * 1024 * 1024


def _cparams(*sem):
    return pltpu.CompilerParams(dimension_semantics=sem, vmem_limit_bytes=VMEM_LIMIT)


def _rms(x, g):
    return x * lax.rsqrt(jnp.mean(x * x, axis=-1, keepdims=True) + NORM_EPS) * g


def _transpose8(vs):
    sub = lax.broadcasted_iota(jnp.int32, (SUBLANES, LANES), 0)
    vs = list(vs)
    span = SUBLANES // 2
    while span >= 1:
        low = (sub % (2 * span)) < span
        for c in range(SUBLANES):
            if c % (2 * span) < span:
                a, b = vs[c], vs[c + span]
                vs[c] = jnp.where(low, a, pltpu.roll(b, span, axis=0))
                vs[c + span] = jnp.where(low, pltpu.roll(a, SUBLANES - span, axis=0), b)
        span //= 2
    return vs


IN_TM = 512


def _in_proj_kernel(x_ref, pos_ref, g_ref, wg_ref, wa_ref, invf_ref, sga_ref, sgb_ref,
                    gla_ref, att_ref):
    h = _rms(x_ref[...], g_ref[...]).astype(BF16)
    gla_ref[...] = jnp.dot(h, wg_ref[...], preferred_element_type=F32)
    ang = pos_ref[...].astype(F32) * invf_ref[...]
    cos = jnp.cos(ang)
    sin = jnp.sin(ang)
    wide = lambda a: jnp.concatenate([a, a], axis=1)
    sin_a = wide(sin * sga_ref[...])
    sin_b = wide(sin * sgb_ref[...])
    cos = wide(cos)
    half = ROT_DIM // 2
    for c in range(ATT_IN_COLS // MXU_COLS):
        y = jnp.dot(h, wa_ref[:, c * MXU_COLS:(c + 1) * MXU_COLS], preferred_element_type=F32)
        if c < 2 * DSW_WIDTH // MXU_COLS:
            y = (y * cos + pltpu.roll(y, MXU_COLS - half, axis=1) * sin_a
                 + pltpu.roll(y, half, axis=1) * sin_b)
        att_ref[:, c * MXU_COLS:(c + 1) * MXU_COLS] = y


def _in_proj(x2, pos2, g, wg, wa, invf, sga, sgb):
    t = x2.shape[0]
    full = lambda shape: pl.BlockSpec(shape, lambda i: (0, 0))
    return pl.pallas_call(
        _in_proj_kernel,
        grid=(t // IN_TM,),
        in_specs=[pl.BlockSpec((IN_TM, D_MODEL), lambda i: (i, 0)),
                  pl.BlockSpec((IN_TM, 1), lambda i: (i, 0)),
                  full((1, D_MODEL)), full((D_MODEL, GLA_IN_COLS)), full((D_MODEL, ATT_IN_COLS)),
                  full((1, LANES)), full((1, LANES)), full((1, LANES))],
        out_specs=[pl.BlockSpec((IN_TM, GLA_IN_COLS), lambda i: (i, 0)),
                   pl.BlockSpec((IN_TM, ATT_IN_COLS), lambda i: (i, 0))],
        out_shape=[jax.ShapeDtypeStruct((t, GLA_IN_COLS), F32),
                   jax.ShapeDtypeStruct((t, ATT_IN_COLS), F32)],
        compiler_params=_cparams("parallel"),
        name="in_proj",
    )(x2, pos2, g, wg, wa, invf, sga, sgb)


GLA_SB = 512
GLA_UNROLL = 4


def _log_sigmoid(z):
    return jnp.minimum(z, 0.0) - jnp.log(1.0 + jnp.exp(-jnp.abs(z)))


def _gla_kernel(in_ref, wup_ref, bup_ref, gout_ref, o_ref, state_ref):
    @pl.when(pl.program_id(1) == 0)
    def _():
        state_ref[...] = jnp.zeros_like(state_ref)

    c_i = lax.broadcasted_iota(jnp.int32, (GLA_CHUNK, GLA_CHUNK), 0)
    c_j = lax.broadcasted_iota(jnp.int32, (GLA_CHUNK, GLA_CHUNK), 1)
    causal = c_j <= c_i
    tri = causal.astype(F32)
    o_q, o_k, o_v, o_r, o_g = 0, GLA_QK, 2 * GLA_QK, 2 * GLA_QK + GLA_WIDTH, 2 * GLA_QK + 2 * GLA_WIDTH

    def chunk(c):
        r0 = pl.multiple_of(c * GLA_CHUNK, GLA_CHUNK)
        rows = pl.ds(r0, GLA_CHUNK)
        code = in_ref[0, rows, o_g:o_g + GATE_PAD]
        pre = jnp.dot(code, wup_ref[...], preferred_element_type=F32) + bup_ref[...]
        log_a = _log_sigmoid(pre) * (1.0 / GLA_TAU)
        b = jnp.dot(tri, log_a, preferred_element_type=F32,
                    precision=lax.Precision.HIGHEST)
        b_last = b[GLA_CHUNK - 1:GLA_CHUNK, :]
        q = in_ref[0, rows, o_q:o_q + GLA_QK]
        k = in_ref[0, rows, o_k:o_k + GLA_QK]
        q_dec = q * (GLA_DK ** -0.5) * jnp.exp(b)
        k_dec = k * jnp.exp(-b)
        k_end = k * jnp.exp(b_last - b)
        decay = jnp.exp(b_last)
        for h in range(GLA_HEADS):
            ks = slice(h * GLA_DK, (h + 1) * GLA_DK)
            vs = slice(h * GLA_DV, (h + 1) * GLA_DV)
            v = in_ref[0, rows, o_v + h * GLA_DV:o_v + (h + 1) * GLA_DV]
            r = in_ref[0, rows, o_r + h * GLA_DV:o_r + (h + 1) * GLA_DV]
            qd = q_dec[:, ks]
            attn = lax.dot_general(qd, k_dec[:, ks], (((1,), (1,)), ((), ())),
                                   preferred_element_type=F32)
            attn = jnp.where(causal, attn, 0.0)
            st = state_ref[h]
            o = (jnp.dot(attn, v, preferred_element_type=F32)
                 + lax.dot_general(qd, st, (((1,), (1,)), ((), ())), preferred_element_type=F32))
            kv_t = lax.dot_general(v, k_end[:, ks], (((0,), (0,)), ((), ())),
                                   preferred_element_type=F32)
            state_ref[h] = st * decay[:, ks] + kv_t
            o = _rms(o, gout_ref[h:h + 1, :])
            o_ref[0, rows, vs] = o * (r / (1.0 + jnp.exp(-r)))

    def chunks(g, carry):
        for u in range(GLA_UNROLL):
            chunk(g * GLA_UNROLL + u)
        return carry

    lax.fori_loop(0, GLA_SB // (GLA_CHUNK * GLA_UNROLL), chunks, 0)


def _gla(gla_in, wup, bup, gout):
    b, s, _ = gla_in.shape
    full = lambda shape: pl.BlockSpec(shape, lambda i, j: (0,) * len(shape))
    return pl.pallas_call(
        _gla_kernel,
        grid=(b, s // GLA_SB),
        in_specs=[pl.BlockSpec((1, GLA_SB, GLA_IN_COLS), lambda i, j: (i, j, 0)),
                  full((GATE_PAD, GLA_QK)), full((1, GLA_QK)), full((GLA_HEADS, GLA_DV))],
        out_specs=pl.BlockSpec((1, GLA_SB, GLA_WIDTH), lambda i, j: (i, j, 0)),
        out_shape=jax.ShapeDtypeStruct((b, s, GLA_WIDTH), F32),
        scratch_shapes=[pltpu.VMEM((GLA_HEADS, GLA_DV, GLA_DK), F32)],
        compiler_params=_cparams("parallel", "arbitrary"),
        name="gla",
    )(gla_in, wup, bup, gout)


DSW_UNROLL = 4


def _dsw_kernel(q_ref, k_ref, v_ref, o_ref, acc_ref, lse_ref):
    s_len = q_ref.shape[1]
    qi = lax.broadcasted_iota(jnp.int32, (DSW_BLOCK, 2 * DSW_BLOCK), 0)
    ki = lax.broadcasted_iota(jnp.int32, (DSW_BLOCK, 2 * DSW_BLOCK), 1)
    rel = qi - ki
    head0 = lax.broadcasted_iota(jnp.int32, (DSW_BLOCK, LANES), 1) < DSW_HEAD_DIM

    for p, (window, d) in enumerate(DSW_PATTERNS):
        n_back = window // d
        nb = s_len // (d * DSW_BLOCK)

        def block(i, p=p, d=d, n_back=n_back, nb=nb):
            r = i // nb
            n = i % nb
            cur_off = jnp.where(n == 0, 0, DSW_BLOCK)
            q0 = r + d * DSW_BLOCK * n
            k0 = q0 - d * cur_off
            if d == 1:
                qrows = pl.ds(pl.multiple_of(q0, DSW_BLOCK), DSW_BLOCK)
                krows = pl.ds(pl.multiple_of(k0, DSW_BLOCK), 2 * DSW_BLOCK)
            else:
                qrows = pl.ds(q0, DSW_BLOCK, stride=d)
                krows = pl.ds(k0, 2 * DSW_BLOCK, stride=d)
            dist = rel + cur_off
            mask = (dist >= 0) & (dist <= n_back)
            q = q_ref[0, qrows, :]
            k = k_ref[0, krows, :].astype(BF16)
            v = v_ref[0, krows, :].astype(BF16)
            outs, lses = [], []
            for h in range(LANES // DSW_HEAD_DIM):
                qh = jnp.where(head0 if h == 0 else ~head0, q, 0.0).astype(BF16)
                s = lax.dot_general(qh, k, (((1,), (1,)), ((), ())), preferred_element_type=F32)
                s = jnp.where(mask, s * (DSW_HEAD_DIM ** -0.5), -jnp.inf)
                m = jnp.max(s, axis=-1, keepdims=True)
                e = jnp.exp(s - m)
                l = jnp.sum(e, axis=-1, keepdims=True)
                outs.append(jnp.dot(e.astype(BF16), v, preferred_element_type=F32) / l)
                lses.append(m + jnp.log(l))
            acc_ref[p, qrows, :] = jnp.where(head0, outs[0], outs[1])
            lse_ref[p, qrows, :] = jnp.where(head0, lses[0], lses[1])

        def blocks(g, carry, block=block):
            for u in range(DSW_UNROLL):
                block(g * DSW_UNROLL + u)
            return carry

        lax.fori_loop(0, s_len // (DSW_BLOCK * DSW_UNROLL), blocks, 0)

    l1, l2, l3 = lse_ref[0], lse_ref[1], lse_ref[2]
    m = jnp.maximum(jnp.maximum(l1, l2), l3)
    e1, e2, e3 = jnp.exp(l1 - m), jnp.exp(l2 - m), jnp.exp(l3 - m)
    o_ref[0] = (e1 * acc_ref[0] + e2 * acc_ref[1] + e3 * acc_ref[2]) / (e1 + e2 + e3)


def _dsw(att_in):
    b, s, _ = att_in.shape
    n_pat = len(DSW_PATTERNS)
    assert n_pat == 3 and all(s % (d * DSW_BLOCK) == 0 and s // (d * DSW_BLOCK) >= 2
                              for _, d in DSW_PATTERNS)
    n_hp = DSW_WIDTH // LANES
    blk = (1, s, LANES)
    in_spec = lambda off: pl.BlockSpec(blk, lambda i, hp: (i, 0, off + hp))
    return pl.pallas_call(
        _dsw_kernel,
        grid=(b, n_hp),
        in_specs=[in_spec(0), in_spec(n_hp), in_spec(2 * n_hp)],
        out_specs=pl.BlockSpec(blk, lambda i, hp: (i, 0, hp)),
        out_shape=jax.ShapeDtypeStruct((b, s, DSW_WIDTH), F32),
        scratch_shapes=[pltpu.VMEM((n_pat, s, LANES), F32), pltpu.VMEM((n_pat, s, LANES), F32)],
        compiler_params=_cparams("parallel", "parallel"),
        name="dsw",
    )(att_in, att_in, att_in)


MIX_TM = 256
PEER_GROUPS = PEER_HEADS * 2


def _mix_out_kernel(x_ref, oa_ref, ob_ref, wo_ref, gp_ref, wq_ref, sk_ref, x1_ref, h_ref, sc_ref):
    mix = (jnp.dot(oa_ref[...].astype(BF16), wo_ref[:GLA_WIDTH, :], preferred_element_type=F32)
           + jnp.dot(ob_ref[...].astype(BF16), wo_ref[GLA_WIDTH:, :], preferred_element_type=F32))
    x1 = x_ref[...] + mix
    x1_ref[...] = x1
    h = _rms(x1, gp_ref[...])
    for g in range(MIX_TM // SUBLANES):
        rows = slice(g * SUBLANES, (g + 1) * SUBLANES)
        slabs = _transpose8([h[rows, c * LANES:(c + 1) * LANES] for c in range(D_SUB)])
        for j in range(SUBLANES):
            h_ref[g * SUBLANES + j] = slabs[j]
    hb = h.astype(BF16)
    for hd in range(PEER_HEADS):
        cols = slice(hd * 2 * PEER_HALF, (hd + 1) * 2 * PEER_HALF)
        qh = jnp.dot(hb, wq_ref[:, cols], preferred_element_type=F32).astype(BF16)
        for c in range(2):
            sc_ref[2 * hd + c] = lax.dot_general(
                sk_ref[2 * hd + c], qh[:, c * PEER_HALF:(c + 1) * PEER_HALF],
                (((1,), (1,)), ((), ())), preferred_element_type=F32)


def _mix_out(x2, oa, ob, wo, gp, wq, sk):
    t = x2.shape[0]
    tok = lambda w: pl.BlockSpec((MIX_TM, w), lambda i: (i, 0))
    full = lambda shape: pl.BlockSpec(shape, lambda i: (0,) * len(shape))
    return pl.pallas_call(
        _mix_out_kernel,
        grid=(t // MIX_TM,),
        in_specs=[tok(D_MODEL), tok(GLA_WIDTH), tok(DSW_WIDTH),
                  full((D_MODEL, D_MODEL)), full((1, D_MODEL)), full((D_MODEL, PEER_QCOLS)),
                  full((PEER_GROUPS, PEER_N_KEYS, PEER_HALF))],
        out_specs=[tok(D_MODEL), pl.BlockSpec((MIX_TM, D_SUB, LANES), lambda i: (i, 0, 0)),
                   pl.BlockSpec((PEER_GROUPS, PEER_N_KEYS, MIX_TM), lambda i: (0, 0, i))],
        out_shape=[jax.ShapeDtypeStruct((t, D_MODEL), F32),
                   jax.ShapeDtypeStruct((t, D_SUB, LANES), F32),
                   jax.ShapeDtypeStruct((PEER_GROUPS, PEER_N_KEYS, t), F32)],
        compiler_params=_cparams("parallel"),
        name="mix_out",
    )(x2, oa, ob, wo, gp, wq, sk)


TOPK_TT = 128
TOPK_UNROLL = 4


def _top_rows(s, order, payload=None):
    vals, idxs, pays = [], [], []
    for _ in range(PEER_TOPK):
        m = jnp.max(s, axis=0, keepdims=True)
        i = jnp.min(jnp.where(s == m, order, jnp.inf), axis=0, keepdims=True)
        hit = order == i
        if payload is not None:
            pays.append(jnp.max(jnp.where(hit, payload, -1.0), axis=0, keepdims=True))
        s = jnp.where(hit, -jnp.inf, s)
        vals.append(m)
        idxs.append(i)
    out = (jnp.concatenate(vals, axis=0), jnp.concatenate(idxs, axis=0))
    if payload is not None:
        out = out + (jnp.concatenate(pays, axis=0),)
    return out


def _pair_candidates(v0, v1, e0, e1):
    n = PEER_TOPK
    half = SUBLANES // 2
    sub = lax.broadcasted_iota(jnp.int32, (SUBLANES, v0.shape[1]), 0)
    subf = sub.astype(F32)
    lo = sub < half
    sums, flats, experts = [], [], []
    for j in range(5):
        sums.append(v0[0:SUBLANES] + v1[j:j + 1])
        experts.append(e0[0:SUBLANES] + e1[j:j + 1])
        flats.append(subf * n + j)
    sums.append(v0[SUBLANES:n] + v1[0:1])
    experts.append(e0[SUBLANES:n] + e1[0:1])
    flats.append((subf + SUBLANES) * n)
    sums.append(v0[0:1] + v1[SUBLANES:n])
    experts.append(e0[0:1] + e1[SUBLANES:n])
    flats.append(subf + SUBLANES)
    shifted = lambda a: jnp.where(lo, pltpu.roll(a[0:SUBLANES], 3, axis=0), pltpu.roll(a[0:SUBLANES], 7, axis=0))
    unused = (sub % half) == half - 1
    sums.append(jnp.where(unused, -jnp.inf, jnp.where(lo, v0[0:1], v0[1:2]) + shifted(v1)))
    experts.append(jnp.where(lo, e0[0:1], e0[1:2]) + shifted(e1))
    flats.append(jnp.where(unused, 1e9, jnp.where(lo, subf + 5, subf + (n + 1))))
    cat = lambda xs: jnp.concatenate(xs, axis=0)
    return cat(sums), cat(flats), cat(experts)


def _peer_topk_kernel(sc_ref, idx_ref, gate_ref):
    rows = lax.broadcasted_iota(jnp.int32, (PEER_N_KEYS, TOPK_TT), 0).astype(F32)

    def head(h):
        v0, i0 = _top_rows(sc_ref[2 * h], rows)
        v1, i1 = _top_rows(sc_ref[2 * h + 1], rows)
        cand, flat, cexp = _pair_candidates(v0, v1, i0 * float(PEER_N_KEYS), i1)
        best, _, expert = _top_rows(cand, flat, cexp)
        e = jnp.exp(best - best[0:1, :])
        gates = e / jnp.sum(e, axis=0, keepdims=True)
        r0 = pl.multiple_of(h * PEER_TOPK, PEER_TOPK)
        idx_ref[pl.ds(r0, PEER_TOPK), :] = expert.astype(jnp.int32)
        gate_ref[pl.ds(r0, PEER_TOPK), :] = gates

    def heads(g, carry):
        for u in range(TOPK_UNROLL):
            head(g * TOPK_UNROLL + u)
        return carry

    lax.fori_loop(0, PEER_HEADS // TOPK_UNROLL, heads, 0)


def _peer_topk(scores_t):
    t = scores_t.shape[-1]
    out_spec = pl.BlockSpec((PEER_PICKS, TOPK_TT), lambda i: (0, i))
    return pl.pallas_call(
        _peer_topk_kernel,
        grid=(t // TOPK_TT,),
        in_specs=[pl.BlockSpec((PEER_GROUPS, PEER_N_KEYS, TOPK_TT), lambda i: (0, 0, i))],
        out_specs=[out_spec, out_spec],
        out_shape=[jax.ShapeDtypeStruct((PEER_PICKS, t), jnp.int32),
                   jax.ShapeDtypeStruct((PEER_PICKS, t), F32)],
        compiler_params=_cparams("parallel"),
        name="peer_topk",
    )(scores_t)


PACK_TE = 512


def _pack_tables_kernel(u_ref, v_ref, o_ref):
    for g in range(PACK_TE // SUBLANES):
        rows = slice(g * SUBLANES, (g + 1) * SUBLANES)
        for half, ref in enumerate((u_ref, v_ref)):
            slabs = _transpose8([ref[rows, c * LANES:(c + 1) * LANES] for c in range(D_SUB)])
            for j in range(SUBLANES):
                o_ref[g * SUBLANES + j, half * D_SUB:(half + 1) * D_SUB, :] = slabs[j]


def _pack_tables(u, v):
    n_exp = u.shape[0]
    tab = pl.BlockSpec((PACK_TE, D_MODEL), lambda i: (i, 0))
    return pl.pallas_call(
        _pack_tables_kernel,
        grid=(n_exp // PACK_TE,),
        in_specs=[tab, tab],
        out_specs=pl.BlockSpec((PACK_TE, ROW_SUB, LANES), lambda i: (i, 0, 0)),
        out_shape=jax.ShapeDtypeStruct((n_exp, ROW_SUB, LANES), F32),
        compiler_params=_cparams("parallel"),
        name="pack_tables",
    )(u, v)


FFN_TB = 64
FFN_NBUF = 16
FFN_AHEAD = 14


def _sublane_fold(vs):
    sub = lax.broadcasted_iota(jnp.int32, (SUBLANES, LANES), 0)
    span = SUBLANES // 2
    while len(vs) > 1:
        low = (sub % (2 * span)) < span
        nxt = []
        for j in range(len(vs) // 2):
            a, b = vs[j], vs[j + len(vs) // 2]
            if span == SUBLANES // 2:
                nxt.append(jnp.where(low, a, b) + pltpu.roll(jnp.where(low, b, a), span, axis=0))
            else:
                nxt.append(jnp.where(low, a + pltpu.roll(a, SUBLANES - span, axis=0),
                                     b + pltpu.roll(b, span, axis=0)))
        vs = nxt
        span //= 2
    return vs[0]


def _peer_ffn_kernel(idx_ref, idx_next_ref, h_ref, gt_ref, uv_ref, o_ref, buf_e, buf_o, act_ref, sems):
    step = pl.program_id(0)
    last = pl.num_programs(0) - 1
    ahead = FFN_AHEAD
    bufs = (buf_e, buf_o)
    pairs_per_buf = FFN_NBUF // 4

    def ring_pos(t):
        return ((t // 4) % pairs_per_buf) * 2 + t % 2

    def row_copy(src_idx_ref, row, k, t_dst, parity):
        return pltpu.make_async_copy(uv_ref.at[src_idx_ref[row, k]],
                                     bufs[parity].at[ring_pos(t_dst), k], sems.at[t_dst % FFN_NBUF])

    def wait(t, parity):
        pltpu.make_async_copy(uv_ref.at[pl.ds(0, PEER_PICKS)], bufs[parity].at[ring_pos(t)],
                              sems.at[t % FFN_NBUF]).wait()

    @pl.when(step == 0)
    def _():
        for t in range(ahead):
            def prime(k2, carry, t=t):
                row_copy(idx_ref, t, 2 * k2, t, (t // 2) % 2).start(priority=0)
                row_copy(idx_ref, t, 2 * k2 + 1, t, (t // 2) % 2).start(priority=1)
                return carry

            lax.fori_loop(0, PEER_PICKS // 2, prime, 0)

    lane = lax.broadcasted_iota(jnp.int32, (PEER_PICKS, LANES), 1)
    lane0 = (step * FFN_TB) % LANES

    def token_pair(t0, parity, src_idx_ref, src_row0):
        buf = bufs[parity]
        slots = (ring_pos(t0), ring_pos(t0 + 1))
        n_half = PEER_PICKS // 2

        def issue(which, k):
            row_copy(src_idx_ref, src_row0 + which, k, t0 + which + ahead,
                     1 - parity).start(priority=k % 2)

        def dots(which):
            x = h_ref[t0 + which]
            packed = []
            for g in range(PEER_PICKS // SUBLANES):
                prods = []
                for j in range(SUBLANES):
                    k = g * SUBLANES + j
                    if k % 2 == 0:
                        issue(which, k // 2)
                    prods.append(buf[slots[which], k, 0:D_SUB, :] * x)
                packed.append(_sublane_fold(prods))
            part = jnp.concatenate(packed, axis=0)
            a = jnp.sum(part, axis=-1, keepdims=True)
            gate = jnp.sum(jnp.where(lane == lane0 + t0 + which, gt_ref[...], 0.0),
                           axis=-1, keepdims=True)
            act = 0.5 * a * (1.0 + lax.erf(a * (2.0 ** -0.5))) * gate
            act_ref[which] = jnp.broadcast_to(act, (PEER_PICKS, LANES))

        def weighted_sum(which):
            acc = jnp.zeros((D_SUB, LANES), F32)
            for k in range(PEER_PICKS):
                if k % 2 == 0:
                    issue(1 - which, n_half + k // 2)
                acc = acc + act_ref[which, k:k + 1, :] * buf[slots[which], k, D_SUB:ROW_SUB, :]
            o_ref[t0 + which] = acc

        wait(t0, parity)
        wait(t0 + 1, parity)
        dots(0)
        dots(1)
        weighted_sum(0)
        weighted_sum(1)

    def quad(i, src_a, src_b):
        t0 = 4 * i
        token_pair(t0, 0, src_a[0], t0 + ahead + src_a[1])
        token_pair(t0 + 2, 1, src_b[0], t0 + 2 + ahead + src_b[1])

    here, nxt = (idx_ref, 0), (idx_next_ref, -FFN_TB)
    n_quads = FFN_TB // 4
    split = (FFN_TB - ahead) // 4

    def same_step(i, carry):
        quad(i, here, here)
        return carry

    def next_step(i, carry):
        quad(i, nxt, nxt)
        return carry

    lax.fori_loop(0, split, same_step, 0)
    quad(split, here, nxt)
    lax.fori_loop(split + 1, n_quads, next_step, 0)

    @pl.when(step == last)
    def _():
        for t in range(ahead):
            wait(t, (t // 2) % 2)


def _peer_ffn(idx, h3, gates_t, uv):
    t = idx.shape[0]
    n_steps = t // FFN_TB
    assert FFN_TB % FFN_NBUF == 0 and FFN_NBUF % 4 == 0 and LANES % FFN_TB == 0
    assert FFN_AHEAD % 4 == 2 and FFN_AHEAD <= FFN_NBUF - 2
    idx_spec = lambda index_map: pl.BlockSpec((FFN_TB, PEER_PICKS), index_map, memory_space=pltpu.SMEM)
    return pl.pallas_call(
        _peer_ffn_kernel,
        grid=(n_steps,),
        in_specs=[idx_spec(lambda i: (i, 0)),
                  idx_spec(lambda i: (jnp.minimum(i + 1, n_steps - 1), 0)),
                  pl.BlockSpec((FFN_TB, D_SUB, LANES), lambda i: (i, 0, 0)),
                  pl.BlockSpec((PEER_PICKS, LANES), lambda i: (0, i * FFN_TB // LANES)),
                  pl.BlockSpec(memory_space=pl.ANY)],
        out_specs=pl.BlockSpec((FFN_TB, D_SUB, LANES), lambda i: (i, 0, 0)),
        out_shape=jax.ShapeDtypeStruct((t, D_SUB, LANES), F32),
        scratch_shapes=[pltpu.VMEM((FFN_NBUF // 2, PEER_PICKS, ROW_SUB, LANES), F32),
                        pltpu.VMEM((FFN_NBUF // 2, PEER_PICKS, ROW_SUB, LANES), F32),
                        pltpu.VMEM((2, PEER_PICKS, LANES), F32),
                        pltpu.SemaphoreType.DMA((FFN_NBUF,))],
        compiler_params=_cparams("arbitrary"),
        name="peer_ffn",
    )(idx, idx, h3, gates_t, uv)


FIN_TM = 256


def _final_kernel(x1_ref, peer_ref, p_ref, gple_ref, wg_ref, wp_ref, gfin_ref, o_ref):
    groups = []
    for g in range(FIN_TM // SUBLANES):
        cols = _transpose8([peer_ref[g * SUBLANES + j] for j in range(SUBLANES)])
        groups.append(jnp.concatenate(cols, axis=1))
    x2 = x1_ref[...] + jnp.concatenate(groups, axis=0)
    z = jnp.dot(_rms(x2, gple_ref[...]).astype(BF16), wg_ref[...], preferred_element_type=F32)
    gate = 1.0 / (1.0 + jnp.exp(-z))
    emb = jnp.dot(p_ref[...].astype(BF16), wp_ref[...], preferred_element_type=F32)
    o_ref[...] = _rms(x2 + emb * gate, gfin_ref[...])


def _final(x1, peer_out, p2, gple, wg, wp, gfin):
    t = x1.shape[0]
    tok = lambda w: pl.BlockSpec((FIN_TM, w), lambda i: (i, 0))
    full = lambda shape: pl.BlockSpec(shape, lambda i: (0, 0))
    return pl.pallas_call(
        _final_kernel,
        grid=(t // FIN_TM,),
        in_specs=[tok(D_MODEL), pl.BlockSpec((FIN_TM, D_SUB, LANES), lambda i: (i, 0, 0)),
                  tok(PLE_DIM), full((1, D_MODEL)),
                  full((D_MODEL, D_MODEL)), full((PLE_DIM, D_MODEL)), full((1, D_MODEL))],
        out_specs=tok(D_MODEL),
        out_shape=jax.ShapeDtypeStruct((t, D_MODEL), F32),
        compiler_params=_cparams("parallel"),
        name="final",
    )(x1, peer_out, p2, gple, wg, wp, gfin)


def _rope_lane_constants():
    lane = np.arange(LANES) % DSW_HEAD_DIM
    half = ROT_DIM // 2
    inv_freq = ROPE_THETA ** (-jnp.arange(0, ROT_DIM, 2, dtype=F32) / ROT_DIM)
    invf = jnp.where(lane < ROT_DIM, inv_freq[lane % half], 0.0).astype(F32)[None, :]
    sga = jnp.asarray(np.where(lane < half, -1.0, 0.0), F32)[None, :]
    sgb = jnp.asarray(np.where((lane >= half) & (lane < ROT_DIM), 1.0, 0.0), F32)[None, :]
    return invf, sga, sgb


def kernel(x, p, positions, g_mix, w_in, w_gla_gate_up, b_gla_gate, g_gla_out, w_out, g_peer,
           w_peer_q, peer_sub_keys, peer_u, peer_v, g_ple, w_ple_gate, w_ple_proj, g_final):
    bsz, slen, dm = x.shape
    t = bsz * slen
    x2 = x.reshape(t, dm)
    pos2 = positions.reshape(t, 1)

    w = w_in[0]
    c_gate = 2 * GLA_QK + 2 * GLA_WIDTH
    w_gla = jnp.concatenate(
        [w[:, :c_gate], jnp.pad(w[:, c_gate:c_gate + GLA_GATE_RANK], ((0, 0), (0, GATE_PAD - GLA_GATE_RANK)))],
        axis=1).astype(BF16)
    w_att = w[:, c_gate + GLA_GATE_RANK:].astype(BF16)
    invf, sga, sgb = _rope_lane_constants()
    gla_in, att_in = _in_proj(x2, pos2, g_mix[0][None, :], w_gla, w_att, invf, sga, sgb)

    wup = jnp.pad(w_gla_gate_up[0], ((0, GATE_PAD - GLA_GATE_RANK), (0, 0)))
    oa = _gla(gla_in.reshape(bsz, slen, GLA_IN_COLS), wup, b_gla_gate[0][None, :], g_gla_out[0])

    ob = _dsw(att_in.reshape(bsz, slen, ATT_IN_COLS))

    sub_keys = peer_sub_keys[0].reshape(PEER_GROUPS, PEER_N_KEYS, PEER_HALF).astype(BF16)
    x1, h, scores_t = _mix_out(
        x2, oa.reshape(t, GLA_WIDTH), ob.reshape(t, DSW_WIDTH),
        w_out[0].astype(BF16), g_peer[0][None, :], w_peer_q[0].astype(BF16), sub_keys)

    idx_t, gates_t = _peer_topk(scores_t)
    peer_out = _peer_ffn(idx_t.T, h, gates_t, _pack_tables(peer_u[0], peer_v[0]))

    out = _final(x1, peer_out, p[0].reshape(t, PLE_DIM), g_ple[0][None, :],
                 w_ple_gate[0].astype(BF16), w_ple_proj[0].astype(BF16), g_final[None, :])
    return out.reshape(bsz, slen, dm)
```

```python
import jax
import jax.numpy as jnp
import numpy as np
from jax import lax
from jax.experimental import pallas as pl
from jax.experimental.pallas import tpu as pltpu

F32 = jnp.float32
BF16 = jnp.bfloat16

D_MODEL = 1024
PLE_DIM = 256
NORM_EPS = 1e-6
GLA_HEADS = 4
GLA_DK = 64
GLA_DV = 128
GLA_QK = GLA_HEADS * GLA_DK
GLA_WIDTH = GLA_HEADS * GLA_DV
GLA_GATE_RANK = 16
GLA_TAU = 16.0
GLA_CHUNK = 64
DSW_HEAD_DIM = 64
DSW_HEADS = 8
DSW_WIDTH = DSW_HEADS * DSW_HEAD_DIM
DSW_PATTERNS = ((128, 1), (512, 4), (2048, 16))
DSW_BLOCK = 128
ROT_DIM = 16
ROPE_THETA = 500000.0
PEER_N_KEYS = 128
PEER_HEADS = 8
PEER_TOPK = 16
PEER_HALF = 128
PEER_QCOLS = PEER_HEADS * 2 * PEER_HALF
PEER_PICKS = PEER_HEADS * PEER_TOPK

LANES = 128
SUBLANES = 8
MXU_COLS = 256
GATE_PAD = LANES
GLA_IN_COLS = 2 * GLA_QK + 2 * GLA_WIDTH + GATE_PAD
ATT_IN_COLS = 3 * DSW_WIDTH
D_SUB = D_MODEL // LANES
ROW_SUB = 2 * D_SUB

VMEM_LIMIT = 56 * 1024 * 1024


def _cparams(*sem):
    return pltpu.CompilerParams(dimension_semantics=sem, vmem_limit_bytes=VMEM_LIMIT)


def _rms(x, g):
    return x * lax.rsqrt(jnp.mean(x * x, axis=-1, keepdims=True) + NORM_EPS) * g


def _transpose8(vs):
    sub = lax.broadcasted_iota(jnp.int32, (SUBLANES, LANES), 0)
    vs = list(vs)
    span = SUBLANES // 2
    while span >= 1:
        low = (sub % (2 * span)) < span
        for c in range(SUBLANES):
            if c % (2 * span) < span:
                a, b = vs[c], vs[c + span]
                vs[c] = jnp.where(low, a, pltpu.roll(b, span, axis=0))
                vs[c + span] = jnp.where(low, pltpu.roll(a, SUBLANES - span, axis=0), b)
        span //= 2
    return vs


IN_TM = 512


def _in_proj_kernel(x_ref, pos_ref, g_ref, wg_ref, wa_ref, invf_ref, sga_ref, sgb_ref,
                    gla_ref, att_ref):
    h = _rms(x_ref[...], g_ref[...]).astype(BF16)
    gla_ref[...] = jnp.dot(h, wg_ref[...], preferred_element_type=F32)
    ang = pos_ref[...].astype(F32) * invf_ref[...]
    cos = jnp.cos(ang)
    sin = jnp.sin(ang)
    wide = lambda a: jnp.concatenate([a, a], axis=1)
    sin_a = wide(sin * sga_ref[...])
    sin_b = wide(sin * sgb_ref[...])
    cos = wide(cos)
    half = ROT_DIM // 2
    for c in range(ATT_IN_COLS // MXU_COLS):
        y = jnp.dot(h, wa_ref[:, c * MXU_COLS:(c + 1) * MXU_COLS], preferred_element_type=F32)
        if c < 2 * DSW_WIDTH // MXU_COLS:
            y = (y * cos + pltpu.roll(y, MXU_COLS - half, axis=1) * sin_a
                 + pltpu.roll(y, half, axis=1) * sin_b)
        att_ref[:, c * MXU_COLS:(c + 1) * MXU_COLS] = y


def _in_proj(x2, pos2, g, wg, wa, invf, sga, sgb):
    t = x2.shape[0]
    full = lambda shape: pl.BlockSpec(shape, lambda i: (0, 0))
    return pl.pallas_call(
        _in_proj_kernel,
        grid=(t // IN_TM,),
        in_specs=[pl.BlockSpec((IN_TM, D_MODEL), lambda i: (i, 0)),
                  pl.BlockSpec((IN_TM, 1), lambda i: (i, 0)),
                  full((1, D_MODEL)), full((D_MODEL, GLA_IN_COLS)), full((D_MODEL, ATT_IN_COLS)),
                  full((1, LANES)), full((1, LANES)), full((1, LANES))],
        out_specs=[pl.BlockSpec((IN_TM, GLA_IN_COLS), lambda i: (i, 0)),
                   pl.BlockSpec((IN_TM, ATT_IN_COLS), lambda i: (i, 0))],
        out_shape=[jax.ShapeDtypeStruct((t, GLA_IN_COLS), F32),
                   jax.ShapeDtypeStruct((t, ATT_IN_COLS), F32)],
        compiler_params=_cparams("parallel"),
        name="in_proj",
    )(x2, pos2, g, wg, wa, invf, sga, sgb)


GLA_SB = 512
GLA_UNROLL = 4


def _log_sigmoid(z):
    return jnp.minimum(z, 0.0) - jnp.log(1.0 + jnp.exp(-jnp.abs(z)))


def _gla_kernel(in_ref, wup_ref, bup_ref, gout_ref, o_ref, state_ref):
    @pl.when(pl.program_id(1) == 0)
    def _():
        state_ref[...] = jnp.zeros_like(state_ref)

    c_i = lax.broadcasted_iota(jnp.int32, (GLA_CHUNK, GLA_CHUNK), 0)
    c_j = lax.broadcasted_iota(jnp.int32, (GLA_CHUNK, GLA_CHUNK), 1)
    causal = c_j <= c_i
    tri = causal.astype(F32)
    o_q, o_k, o_v, o_r, o_g = 0, GLA_QK, 2 * GLA_QK, 2 * GLA_QK + GLA_WIDTH, 2 * GLA_QK + 2 * GLA_WIDTH

    def chunk(c):
        r0 = pl.multiple_of(c * GLA_CHUNK, GLA_CHUNK)
        rows = pl.ds(r0, GLA_CHUNK)
        code = in_ref[0, rows, o_g:o_g + GATE_PAD]
        pre = jnp.dot(code, wup_ref[...], preferred_element_type=F32) + bup_ref[...]
        log_a = _log_sigmoid(pre) * (1.0 / GLA_TAU)
        b = jnp.dot(tri, log_a, preferred_element_type=F32,
                    precision=lax.Precision.HIGHEST)
        b_last = b[GLA_CHUNK - 1:GLA_CHUNK, :]
        q = in_ref[0, rows, o_q:o_q + GLA_QK]
        k = in_ref[0, rows, o_k:o_k + GLA_QK]
        q_dec = q * (GLA_DK ** -0.5) * jnp.exp(b)
        k_dec = k * jnp.exp(-b)
        k_end = k * jnp.exp(b_last - b)
        decay = jnp.exp(b_last)
        for h in range(GLA_HEADS):
            ks = slice(h * GLA_DK, (h + 1) * GLA_DK)
            vs = slice(h * GLA_DV, (h + 1) * GLA_DV)
            v = in_ref[0, rows, o_v + h * GLA_DV:o_v + (h + 1) * GLA_DV]
            r = in_ref[0, rows, o_r + h * GLA_DV:o_r + (h + 1) * GLA_DV]
            qd = q_dec[:, ks]
            attn = lax.dot_general(qd, k_dec[:, ks], (((1,), (1,)), ((), ())),
                                   preferred_element_type=F32)
            attn = jnp.where(causal, attn, 0.0)
            st = state_ref[h]
            o = (jnp.dot(attn, v, preferred_element_type=F32)
                 + lax.dot_general(qd, st, (((1,), (1,)), ((), ())), preferred_element_type=F32))
            kv_t = lax.dot_general(v, k_end[:, ks], (((0,), (0,)), ((), ())),
                                   preferred_element_type=F32)
            state_ref[h] = st * decay[:, ks] + kv_t
            o = _rms(o, gout_ref[h:h + 1, :])
            o_ref[0, rows, vs] = o * (r / (1.0 + jnp.exp(-r)))

    def chunks(g, carry):
        for u in range(GLA_UNROLL):
            chunk(g * GLA_UNROLL + u)
        return carry

    lax.fori_loop(0, GLA_SB // (GLA_CHUNK * GLA_UNROLL), chunks, 0)


def _gla(gla_in, wup, bup, gout):
    b, s, _ = gla_in.shape
    full = lambda shape: pl.BlockSpec(shape, lambda i, j: (0,) * len(shape))
    return pl.pallas_call(
        _gla_kernel,
        grid=(b, s // GLA_SB),
        in_specs=[pl.BlockSpec((1, GLA_SB, GLA_IN_COLS), lambda i, j: (i, j, 0)),
                  full((GATE_PAD, GLA_QK)), full((1, GLA_QK)), full((GLA_HEADS, GLA_DV))],
        out_specs=pl.BlockSpec((1, GLA_SB, GLA_WIDTH), lambda i, j: (i, j, 0)),
        out_shape=jax.ShapeDtypeStruct((b, s, GLA_WIDTH), F32),
        scratch_shapes=[pltpu.VMEM((GLA_HEADS, GLA_DV, GLA_DK), F32)],
        compiler_params=_cparams("parallel", "arbitrary"),
        name="gla",
    )(gla_in, wup, bup, gout)


DSW_UNROLL = 4


def _dsw_kernel(q_ref, k_ref, v_ref, o_ref, acc_ref, lse_ref):
    s_len = q_ref.shape[1]
    qi = lax.broadcasted_iota(jnp.int32, (DSW_BLOCK, 2 * DSW_BLOCK), 0)
    ki = lax.broadcasted_iota(jnp.int32, (DSW_BLOCK, 2 * DSW_BLOCK), 1)
    rel = qi - ki
    head0 = lax.broadcasted_iota(jnp.int32, (DSW_BLOCK, LANES), 1) < DSW_HEAD_DIM

    for p, (window, d) in enumerate(DSW_PATTERNS):
        n_back = window // d
        nb = s_len // (d * DSW_BLOCK)

        def block(i, p=p, d=d, n_back=n_back, nb=nb):
            r = i // nb
            n = i % nb
            cur_off = jnp.where(n == 0, 0, DSW_BLOCK)
            q0 = r + d * DSW_BLOCK * n
            k0 = q0 - d * cur_off
            if d == 1:
                qrows = pl.ds(pl.multiple_of(q0, DSW_BLOCK), DSW_BLOCK)
                krows = pl.ds(pl.multiple_of(k0, DSW_BLOCK), 2 * DSW_BLOCK)
            else:
                qrows = pl.ds(q0, DSW_BLOCK, stride=d)
                krows = pl.ds(k0, 2 * DSW_BLOCK, stride=d)
            dist = rel + cur_off
            mask = (dist >= 0) & (dist <= n_back)
            q = q_ref[0, qrows, :]
            k = k_ref[0, krows, :].astype(BF16)
            v = v_ref[0, krows, :].astype(BF16)
            outs, lses = [], []
            for h in range(LANES // DSW_HEAD_DIM):
                qh = jnp.where(head0 if h == 0 else ~head0, q, 0.0).astype(BF16)
                s = lax.dot_general(qh, k, (((1,), (1,)), ((), ())), preferred_element_type=F32)
                s = jnp.where(mask, s * (DSW_HEAD_DIM ** -0.5), -jnp.inf)
                m = jnp.max(s, axis=-1, keepdims=True)
                e = jnp.exp(s - m)
                l = jnp.sum(e, axis=-1, keepdims=True)
                outs.append(jnp.dot(e.astype(BF16), v, preferred_element_type=F32) / l)
                lses.append(m + jnp.log(l))
            acc_ref[p, qrows, :] = jnp.where(head0, outs[0], outs[1])
            lse_ref[p, qrows, :] = jnp.where(head0, lses[0], lses[1])

        def blocks(g, carry, block=block):
            for u in range(DSW_UNROLL):
                block(g * DSW_UNROLL + u)
            return carry

        lax.fori_loop(0, s_len // (DSW_BLOCK * DSW_UNROLL), blocks, 0)

    l1, l2, l3 = lse_ref[0], lse_ref[1], lse_ref[2]
    m = jnp.maximum(jnp.maximum(l1, l2), l3)
    e1, e2, e3 = jnp.exp(l1 - m), jnp.exp(l2 - m), jnp.exp(l3 - m)
    o_ref[0] = (e1 * acc_ref[0] + e2 * acc_ref[1] + e3 * acc_ref[2]) / (e1 + e2 + e3)


def _dsw(att_in):
    b, s, _ = att_in.shape
    n_pat = len(DSW_PATTERNS)
    assert n_pat == 3 and all(s % (d * DSW_BLOCK) == 0 and s // (d * DSW_BLOCK) >= 2
                              for _, d in DSW_PATTERNS)
    n_hp = DSW_WIDTH // LANES
    blk = (1, s, LANES)
    in_spec = lambda off: pl.BlockSpec(blk, lambda i, hp: (i, 0, off + hp))
    return pl.pallas_call(
        _dsw_kernel,
        grid=(b, n_hp),
        in_specs=[in_spec(0), in_spec(n_hp), in_spec(2 * n_hp)],
        out_specs=pl.BlockSpec(blk, lambda i, hp: (i, 0, hp)),
        out_shape=jax.ShapeDtypeStruct((b, s, DSW_WIDTH), F32),
        scratch_shapes=[pltpu.VMEM((n_pat, s, LANES), F32), pltpu.VMEM((n_pat, s, LANES), F32)],
        compiler_params=_cparams("parallel", "parallel"),
        name="dsw",
    )(att_in, att_in, att_in)


MIX_TM = 256
PEER_GROUPS = PEER_HEADS * 2


def _mix_out_kernel(x_ref, oa_ref, ob_ref, wo_ref, gp_ref, wq_ref, sk_ref, x1_ref, h_ref, sc_ref):
    mix = (jnp.dot(oa_ref[...].astype(BF16), wo_ref[:GLA_WIDTH, :], preferred_element_type=F32)
           + jnp.dot(ob_ref[...].astype(BF16), wo_ref[GLA_WIDTH:, :], preferred_element_type=F32))
    x1 = x_ref[...] + mix
    x1_ref[...] = x1
    h = _rms(x1, gp_ref[...])
    for g in range(MIX_TM // SUBLANES):
        rows = slice(g * SUBLANES, (g + 1) * SUBLANES)
        slabs = _transpose8([h[rows, c * LANES:(c + 1) * LANES] for c in range(D_SUB)])
        for j in range(SUBLANES):
            h_ref[g * SUBLANES + j] = slabs[j]
    hb = h.astype(BF16)
    for hd in range(PEER_HEADS):
        cols = slice(hd * 2 * PEER_HALF, (hd + 1) * 2 * PEER_HALF)
        qh = jnp.dot(hb, wq_ref[:, cols], preferred_element_type=F32).astype(BF16)
        for c in range(2):
            sc_ref[2 * hd + c] = lax.dot_general(
                sk_ref[2 * hd + c], qh[:, c * PEER_HALF:(c + 1) * PEER_HALF],
                (((1,), (1,)), ((), ())), preferred_element_type=F32)


def _mix_out(x2, oa, ob, wo, gp, wq, sk):
    t = x2.shape[0]
    tok = lambda w: pl.BlockSpec((MIX_TM, w), lambda i: (i, 0))
    full = lambda shape: pl.BlockSpec(shape, lambda i: (0,) * len(shape))
    return pl.pallas_call(
        _mix_out_kernel,
        grid=(t // MIX_TM,),
        in_specs=[tok(D_MODEL), tok(GLA_WIDTH), tok(DSW_WIDTH),
                  full((D_MODEL, D_MODEL)), full((1, D_MODEL)), full((D_MODEL, PEER_QCOLS)),
                  full((PEER_GROUPS, PEER_N_KEYS, PEER_HALF))],
        out_specs=[tok(D_MODEL), pl.BlockSpec((MIX_TM, D_SUB, LANES), lambda i: (i, 0, 0)),
                   pl.BlockSpec((PEER_GROUPS, PEER_N_KEYS, MIX_TM), lambda i: (0, 0, i))],
        out_shape=[jax.ShapeDtypeStruct((t, D_MODEL), F32),
                   jax.ShapeDtypeStruct((t, D_SUB, LANES), F32),
                   jax.ShapeDtypeStruct((PEER_GROUPS, PEER_N_KEYS, t), F32)],
        compiler_params=_cparams("parallel"),
        name="mix_out",
    )(x2, oa, ob, wo, gp, wq, sk)


TOPK_TT = 128
TOPK_UNROLL = 4


def _top_rows(s, order, payload=None):
    vals, idxs, pays = [], [], []
    for _ in range(PEER_TOPK):
        m = jnp.max(s, axis=0, keepdims=True)
        i = jnp.min(jnp.where(s == m, order, jnp.inf), axis=0, keepdims=True)
        hit = order == i
        if payload is not None:
            pays.append(jnp.max(jnp.where(hit, payload, -1.0), axis=0, keepdims=True))
        s = jnp.where(hit, -jnp.inf, s)
        vals.append(m)
        idxs.append(i)
    out = (jnp.concatenate(vals, axis=0), jnp.concatenate(idxs, axis=0))
    if payload is not None:
        out = out + (jnp.concatenate(pays, axis=0),)
    return out


def _pair_candidates(v0, v1, e0, e1):
    n = PEER_TOPK
    half = SUBLANES // 2
    sub = lax.broadcasted_iota(jnp.int32, (SUBLANES, v0.shape[1]), 0)
    subf = sub.astype(F32)
    lo = sub < half
    sums, flats, experts = [], [], []
    for j in range(5):
        sums.append(v0[0:SUBLANES] + v1[j:j + 1])
        experts.append(e0[0:SUBLANES] + e1[j:j + 1])
        flats.append(subf * n + j)
    sums.append(v0[SUBLANES:n] + v1[0:1])
    experts.append(e0[SUBLANES:n] + e1[0:1])
    flats.append((subf + SUBLANES) * n)
    sums.append(v0[0:1] + v1[SUBLANES:n])
    experts.append(e0[0:1] + e1[SUBLANES:n])
    flats.append(subf + SUBLANES)
    shifted = lambda a: jnp.where(lo, pltpu.roll(a[0:SUBLANES], 3, axis=0), pltpu.roll(a[0:SUBLANES], 7, axis=0))
    unused = (sub % half) == half - 1
    sums.append(jnp.where(unused, -jnp.inf, jnp.where(lo, v0[0:1], v0[1:2]) + shifted(v1)))
    experts.append(jnp.where(lo, e0[0:1], e0[1:2]) + shifted(e1))
    flats.append(jnp.where(unused, 1e9, jnp.where(lo, subf + 5, subf + (n + 1))))
    cat = lambda xs: jnp.concatenate(xs, axis=0)
    return cat(sums), cat(flats), cat(experts)


def _peer_topk_kernel(sc_ref, idx_ref, gate_ref):
    rows = lax.broadcasted_iota(jnp.int32, (PEER_N_KEYS, TOPK_TT), 0).astype(F32)

    def head(h):
        v0, i0 = _top_rows(sc_ref[2 * h], rows)
        v1, i1 = _top_rows(sc_ref[2 * h + 1], rows)
        cand, flat, cexp = _pair_candidates(v0, v1, i0 * float(PEER_N_KEYS), i1)
        best, _, expert = _top_rows(cand, flat, cexp)
        e = jnp.exp(best - best[0:1, :])
        gates = e / jnp.sum(e, axis=0, keepdims=True)
        r0 = pl.multiple_of(h * PEER_TOPK, PEER_TOPK)
        idx_ref[pl.ds(r0, PEER_TOPK), :] = expert.astype(jnp.int32)
        gate_ref[pl.ds(r0, PEER_TOPK), :] = gates

    def heads(g, carry):
        for u in range(TOPK_UNROLL):
            head(g * TOPK_UNROLL + u)
        return carry

    lax.fori_loop(0, PEER_HEADS // TOPK_UNROLL, heads, 0)


def _peer_topk(scores_t):
    t = scores_t.shape[-1]
    out_spec = pl.BlockSpec((PEER_PICKS, TOPK_TT), lambda i: (0, i))
    return pl.pallas_call(
        _peer_topk_kernel,
        grid=(t // TOPK_TT,),
        in_specs=[pl.BlockSpec((PEER_GROUPS, PEER_N_KEYS, TOPK_TT), lambda i: (0, 0, i))],
        out_specs=[out_spec, out_spec],
        out_shape=[jax.ShapeDtypeStruct((PEER_PICKS, t), jnp.int32),
                   jax.ShapeDtypeStruct((PEER_PICKS, t), F32)],
        compiler_params=_cparams("parallel"),
        name="peer_topk",
    )(scores_t)


PACK_TE = 512


def _pack_tables_kernel(u_ref, v_ref, o_ref):
    for g in range(PACK_TE // SUBLANES):
        rows = slice(g * SUBLANES, (g + 1) * SUBLANES)
        for half, ref in enumerate((u_ref, v_ref)):
            slabs = _transpose8([ref[rows, c * LANES:(c + 1) * LANES] for c in range(D_SUB)])
            for j in range(SUBLANES):
                o_ref[g * SUBLANES + j, half * D_SUB:(half + 1) * D_SUB, :] = slabs[j]


def _pack_tables(u, v):
    n_exp = u.shape[0]
    tab = pl.BlockSpec((PACK_TE, D_MODEL), lambda i: (i, 0))
    return pl.pallas_call(
        _pack_tables_kernel,
        grid=(n_exp // PACK_TE,),
        in_specs=[tab, tab],
        out_specs=pl.BlockSpec((PACK_TE, ROW_SUB, LANES), lambda i: (i, 0, 0)),
        out_shape=jax.ShapeDtypeStruct((n_exp, ROW_SUB, LANES), F32),
        compiler_params=_cparams("parallel"),
        name="pack_tables",
    )(u, v)


FFN_TB = 64
FFN_NBUF = 16
FFN_AHEAD = 14


def _sublane_fold(vs):
    sub = lax.broadcasted_iota(jnp.int32, (SUBLANES, LANES), 0)
    span = SUBLANES // 2
    while len(vs) > 1:
        low = (sub % (2 * span)) < span
        nxt = []
        for j in range(len(vs) // 2):
            a, b = vs[j], vs[j + len(vs) // 2]
            if span == SUBLANES // 2:
                nxt.append(jnp.where(low, a, b) + pltpu.roll(jnp.where(low, b, a), span, axis=0))
            else:
                nxt.append(jnp.where(low, a + pltpu.roll(a, SUBLANES - span, axis=0),
                                     b + pltpu.roll(b, span, axis=0)))
        vs = nxt
        span //= 2
    return vs[0]


def _peer_ffn_kernel(idx_ref, idx_next_ref, h_ref, gt_ref, uv_ref, o_ref, buf_e, buf_o, act_ref, sems):
    step = pl.program_id(0)
    last = pl.num_programs(0) - 1
    ahead = FFN_AHEAD
    bufs = (buf_e, buf_o)
    pairs_per_buf = FFN_NBUF // 4

    def ring_pos(t):
        return ((t // 4) % pairs_per_buf) * 2 + t % 2

    def row_copy(src_idx_ref, row, k, t_dst, parity):
        return pltpu.make_async_copy(uv_ref.at[src_idx_ref[row, k]],
                                     bufs[parity].at[ring_pos(t_dst), k], sems.at[t_dst % FFN_NBUF])

    def wait(t, parity):
        pltpu.make_async_copy(uv_ref.at[pl.ds(0, PEER_PICKS)], bufs[parity].at[ring_pos(t)],
                              sems.at[t % FFN_NBUF]).wait()

    @pl.when(step == 0)
    def _():
        for t in range(ahead):
            def prime(k2, carry, t=t):
                row_copy(idx_ref, t, 2 * k2, t, (t // 2) % 2).start(priority=0)
                row_copy(idx_ref, t, 2 * k2 + 1, t, (t // 2) % 2).start(priority=1)
                return carry

            lax.fori_loop(0, PEER_PICKS // 2, prime, 0)

    lane = lax.broadcasted_iota(jnp.int32, (PEER_PICKS, LANES), 1)
    lane0 = (step * FFN_TB) % LANES

    def token_pair(t0, parity, src_idx_ref, src_row0):
        buf = bufs[parity]
        slots = (ring_pos(t0), ring_pos(t0 + 1))
        n_half = PEER_PICKS // 2

        def issue(which, k):
            row_copy(src_idx_ref, src_row0 + which, k, t0 + which + ahead,
                     1 - parity).start(priority=k % 2)

        def dots(which):
            x = h_ref[t0 + which]
            packed = []
            for g in range(PEER_PICKS // SUBLANES):
                prods = []
                for j in range(SUBLANES):
                    k = g * SUBLANES + j
                    if k % 2 == 0:
                        issue(which, k // 2)
                    prods.append(buf[slots[which], k, 0:D_SUB, :] * x)
                packed.append(_sublane_fold(prods))
            part = jnp.concatenate(packed, axis=0)
            a = jnp.sum(part, axis=-1, keepdims=True)
            gate = jnp.sum(jnp.where(lane == lane0 + t0 + which, gt_ref[...], 0.0),
                           axis=-1, keepdims=True)
            act = 0.5 * a * (1.0 + lax.erf(a * (2.0 ** -0.5))) * gate
            act_ref[which] = jnp.broadcast_to(act, (PEER_PICKS, LANES))

        def weighted_sum(which):
            acc = jnp.zeros((D_SUB, LANES), F32)
            for k in range(PEER_PICKS):
                if k % 2 == 0:
                    issue(1 - which, n_half + k // 2)
                acc = acc + act_ref[which, k:k + 1, :] * buf[slots[which], k, D_SUB:ROW_SUB, :]
            o_ref[t0 + which] = acc

        wait(t0, parity)
        wait(t0 + 1, parity)
        dots(0)
        dots(1)
        weighted_sum(0)
        weighted_sum(1)

    def quad(i, src_a, src_b):
        t0 = 4 * i
        token_pair(t0, 0, src_a[0], t0 + ahead + src_a[1])
        token_pair(t0 + 2, 1, src_b[0], t0 + 2 + ahead + src_b[1])

    here, nxt = (idx_ref, 0), (idx_next_ref, -FFN_TB)
    n_quads = FFN_TB // 4
    split = (FFN_TB - ahead) // 4

    def same_step(i, carry):
        quad(i, here, here)
        return carry

    def next_step(i, carry):
        quad(i, nxt, nxt)
        return carry

    lax.fori_loop(0, split, same_step, 0)
    quad(split, here, nxt)
    lax.fori_loop(split + 1, n_quads, next_step, 0)

    @pl.when(step == last)
    def _():
        for t in range(ahead):
            wait(t, (t // 2) % 2)


def _peer_ffn(idx, h3, gates_t, uv):
    t = idx.shape[0]
    n_steps = t // FFN_TB
    assert FFN_TB % FFN_NBUF == 0 and FFN_NBUF % 4 == 0 and LANES % FFN_TB == 0
    assert FFN_AHEAD % 4 == 2 and FFN_AHEAD <= FFN_NBUF - 2
    idx_spec = lambda index_map: pl.BlockSpec((FFN_TB, PEER_PICKS), index_map, memory_space=pltpu.SMEM)
    return pl.pallas_call(
        _peer_ffn_kernel,
        grid=(n_steps,),
        in_specs=[idx_spec(lambda i: (i, 0)),
                  idx_spec(lambda i: (jnp.minimum(i + 1, n_steps - 1), 0)),
                  pl.BlockSpec((FFN_TB, D_SUB, LANES), lambda i: (i, 0, 0)),
                  pl.BlockSpec((PEER_PICKS, LANES), lambda i: (0, i * FFN_TB // LANES)),
                  pl.BlockSpec(memory_space=pl.ANY)],
        out_specs=pl.BlockSpec((FFN_TB, D_SUB, LANES), lambda i: (i, 0, 0)),
        out_shape=jax.ShapeDtypeStruct((t, D_SUB, LANES), F32),
        scratch_shapes=[pltpu.VMEM((FFN_NBUF // 2, PEER_PICKS, ROW_SUB, LANES), F32),
                        pltpu.VMEM((FFN_NBUF // 2, PEER_PICKS, ROW_SUB, LANES), F32),
                        pltpu.VMEM((2, PEER_PICKS, LANES), F32),
                        pltpu.SemaphoreType.DMA((FFN_NBUF,))],
        compiler_params=_cparams("arbitrary"),
        name="peer_ffn",
    )(idx, idx, h3, gates_t, uv)


FIN_TM = 256
FIN_BUFS = 3


def _final_kernel(x1_ref, peer_ref, p_ref, gple_ref, wg_ref, wp_ref, gfin_ref, o_ref):
    groups = []
    for g in range(FIN_TM // SUBLANES):
        cols = _transpose8([peer_ref[g * SUBLANES + j] for j in range(SUBLANES)])
        groups.append(jnp.concatenate(cols, axis=1))
    x2 = x1_ref[...] + jnp.concatenate(groups, axis=0)
    z = jnp.dot(_rms(x2, gple_ref[...]).astype(BF16), wg_ref[...], preferred_element_type=F32)
    gate = 1.0 / (1.0 + jnp.exp(-z))
    emb = jnp.dot(p_ref[...].astype(BF16), wp_ref[...], preferred_element_type=F32)
    o_ref[...] = _rms(x2 + emb * gate, gfin_ref[...])


def _final(x1, peer_out, p2, gple, wg, wp, gfin):
    t = x1.shape[0]
    tok = lambda w: pl.BlockSpec((FIN_TM, w), lambda i: (i, 0))
    full = lambda shape: pl.BlockSpec(shape, lambda i: (0, 0))
    deep = pl.Buffered(FIN_BUFS)

    def outer(x1_hbm, peer_hbm, p_hbm, gple_ref, wg_ref, wp_ref, gfin_ref, o_hbm):
        def body(x1_ref, peer_ref, p_ref, o_ref):
            _final_kernel(x1_ref, peer_ref, p_ref, gple_ref, wg_ref, wp_ref, gfin_ref, o_ref)

        pltpu.emit_pipeline(
            body,
            grid=(t // FIN_TM,),
            in_specs=[pl.BlockSpec((FIN_TM, D_MODEL), lambda i: (i, 0), pipeline_mode=deep),
                      pl.BlockSpec((FIN_TM, D_SUB, LANES), lambda i: (i, 0, 0), pipeline_mode=deep),
                      tok(PLE_DIM)],
            out_specs=[tok(D_MODEL)],
        )(x1_hbm, peer_hbm, p_hbm, o_hbm)

    hbm = pl.BlockSpec(memory_space=pl.ANY)
    vmem = pl.BlockSpec(memory_space=pltpu.VMEM)
    return pl.pallas_call(
        outer,
        in_specs=[hbm, hbm, hbm, vmem, vmem, vmem, vmem],
        out_specs=hbm,
        out_shape=jax.ShapeDtypeStruct((t, D_MODEL), F32),
        compiler_params=pltpu.CompilerParams(vmem_limit_bytes=VMEM_LIMIT),
        name="final",
    )(x1, peer_out, p2, gple, wg, wp, gfin)


def _rope_lane_constants():
    lane = np.arange(LANES) % DSW_HEAD_DIM
    half = ROT_DIM // 2
    inv_freq = ROPE_THETA ** (-jnp.arange(0, ROT_DIM, 2, dtype=F32) / ROT_DIM)
    invf = jnp.where(lane < ROT_DIM, inv_freq[lane % half], 0.0).astype(F32)[None, :]
    sga = jnp.asarray(np.where(lane < half, -1.0, 0.0), F32)[None, :]
    sgb = jnp.asarray(np.where((lane >= half) & (lane < ROT_DIM), 1.0, 0.0), F32)[None, :]
    return invf, sga, sgb


def kernel(x, p, positions, g_mix, w_in, w_gla_gate_up, b_gla_gate, g_gla_out, w_out, g_peer,
           w_peer_q, peer_sub_keys, peer_u, peer_v, g_ple, w_ple_gate, w_ple_proj, g_final):
    bsz, slen, dm = x.shape
    t = bsz * slen
    x2 = x.reshape(t, dm)
    pos2 = positions.reshape(t, 1)

    w = w_in[0]
    c_gate = 2 * GLA_QK + 2 * GLA_WIDTH
    w_gla = jnp.concatenate(
        [w[:, :c_gate], jnp.pad(w[:, c_gate:c_gate + GLA_GATE_RANK], ((0, 0), (0, GATE_PAD - GLA_GATE_RANK)))],
        axis=1).astype(BF16)
    w_att = w[:, c_gate + GLA_GATE_RANK:].astype(BF16)
    invf, sga, sgb = _rope_lane_constants()
    gla_in, att_in = _in_proj(x2, pos2, g_mix[0][None, :], w_gla, w_att, invf, sga, sgb)

    wup = jnp.pad(w_gla_gate_up[0], ((0, GATE_PAD - GLA_GATE_RANK), (0, 0)))
    oa = _gla(gla_in.reshape(bsz, slen, GLA_IN_COLS), wup, b_gla_gate[0][None, :], g_gla_out[0])

    ob = _dsw(att_in.reshape(bsz, slen, ATT_IN_COLS))

    sub_keys = peer_sub_keys[0].reshape(PEER_GROUPS, PEER_N_KEYS, PEER_HALF).astype(BF16)
    x1, h, scores_t = _mix_out(
        x2, oa.reshape(t, GLA_WIDTH), ob.reshape(t, DSW_WIDTH),
        w_out[0].astype(BF16), g_peer[0][None, :], w_peer_q[0].astype(BF16), sub_keys)

    idx_t, gates_t = _peer_topk(scores_t)
    peer_out = _peer_ffn(idx_t.T, h, gates_t, _pack_tables(peer_u[0], peer_v[0]))

    out = _final(x1, peer_out, p[0].reshape(t, PLE_DIM), g_ple[0][None, :],
                 w_ple_gate[0].astype(BF16), w_ple_proj[0].astype(BF16), g_final[None, :])
    return out.reshape(bsz, slen, dm)
```
